```python
import jax, jax.numpy as jnp
from jax import lax
import numpy as np

D_MODEL = 1024
BATCH = 2
SEQ = 8192
DEPTH = 1

DN_HEADS = 4
DN_HEAD_DIM = 128
DN_WIDTH = DN_HEADS * DN_HEAD_DIM
DN_CONV = 5
DN_CHUNK = 64
N_DIR = 2
SWA_Q_HEADS = 8
SWA_KV_HEADS = 2
SWA_HEAD_DIM = 64
SWA_WIDTH = SWA_Q_HEADS * SWA_HEAD_DIM
SWA_KV_WIDTH = SWA_KV_HEADS * SWA_HEAD_DIM
WINDOW = 128
BLOCK = 128
ROPE_THETA = 10000.0
EPS = 1e-6
MIX_WIDTH = DN_WIDTH + SWA_WIDTH
SPLITS = (DN_WIDTH, DN_WIDTH, DN_WIDTH, DN_WIDTH, N_DIR * DN_HEADS, N_DIR * DN_HEADS,
          SWA_WIDTH, SWA_KV_WIDTH, SWA_KV_WIDTH, SWA_WIDTH)
IN_WIDTH = 4 * DN_WIDTH + 2 * N_DIR * DN_HEADS + 2 * SWA_WIDTH + 2 * SWA_KV_WIDTH

kernel_name = 'hybrid_gdn_swa_parallel_heads'


def rms_norm(x, w):
    xf = x.astype(jnp.float32)
    y = xf * lax.rsqrt(jnp.mean(xf * xf, axis=-1, keepdims=True) + EPS)
    return (y * w.astype(jnp.float32)).astype(x.dtype)


def l2_norm(x):
    return x * lax.rsqrt(jnp.sum(x * x, axis=-1, keepdims=True) + EPS)


def rope_tables(seq, dim):
    inv_freq = ROPE_THETA ** (-jnp.arange(0, dim, 2, dtype=jnp.float32) / dim)
    ang = jnp.arange(seq, dtype=jnp.float32)[:, None] * inv_freq[None, :]
    ang = jnp.concatenate([ang, ang], axis=-1)
    return jnp.cos(ang), jnp.sin(ang)


def rotary(x, cos, sin):
    half = x.shape[-1] // 2
    rot = jnp.concatenate([-x[..., half:], x[..., :half]], axis=-1)
    return x * cos[:, None, :] + rot * sin[:, None, :]


def centred_depthwise_conv(x, w):
    pad = (w.shape[0] - 1) // 2
    return lax.conv_general_dilated(
        x, w[:, None, :].astype(x.dtype), window_strides=(1,), padding=[(pad, pad)],
        dimension_numbers=('NWC', 'WIO', 'NWC'), feature_group_count=x.shape[-1])


def gated_delta_rule_chunked(q, k, v, g, beta):
    b, t, h, dk = q.shape
    dv = v.shape[-1]
    c = DN_CHUNK
    n = t // c

    def to_chunks(a):
        return jnp.moveaxis(a.reshape(b, n, c, h, *a.shape[3:]), 3, 1)

    q, k, v, g, beta = (to_chunks(a) for a in (q, k, v, g, beta))
    g = jnp.cumsum(g, axis=-1)
    incl = jnp.tril(jnp.ones((c, c), dtype=bool))
    strict = jnp.tril(jnp.ones((c, c), dtype=bool), -1)
    decay = jnp.exp(jnp.where(incl, g[..., :, None] - g[..., None, :], -jnp.inf))
    k_beta = k * beta[..., None]
    m = jnp.where(strict, jnp.einsum('bhncd,bhnsd->bhncs', k_beta, k) * decay, 0.0)
    eye = jnp.eye(c, dtype=q.dtype)
    t_inv = lax.linalg.triangular_solve(eye + m, jnp.broadcast_to(eye, m.shape),
                                        left_side=True, lower=True, unit_diagonal=True)
    u = jnp.einsum('bhncs,bhnse->bhnce', t_inv, v * beta[..., None])
    w = jnp.einsum('bhncs,bhnsd->bhncd', t_inv, k_beta * jnp.exp(g)[..., None])
    a_intra = jnp.where(incl, jnp.einsum('bhncd,bhnsd->bhncs', q, k) * decay, 0.0)

    def step(state, inp):
        q_i, k_i, u_i, w_i, g_i, a_i = inp
        v_new = u_i - jnp.einsum('bhcd,bhde->bhce', w_i, state)
        o = (jnp.einsum('bhcd,bhde->bhce', q_i * jnp.exp(g_i)[..., None], state)
             + jnp.einsum('bhcs,bhse->bhce', a_i, v_new))
        g_last = g_i[..., -1]
        k_dec = k_i * jnp.exp(g_last[..., None] - g_i)[..., None]
        state = state * jnp.exp(g_last)[..., None, None] + jnp.einsum('bhcd,bhce->bhde', k_dec, v_new)
        return state, o

    xs = tuple(jnp.moveaxis(a, 2, 0) for a in (q, k, u, w, g, a_intra))
    state0 = jnp.zeros((b, h, dk, dv), dtype=q.dtype)
    _, o = lax.scan(step, state0, xs)
    o = jnp.moveaxis(o, 0, 2)
    return jnp.moveaxis(o, 1, 3).reshape(b, t, h, dv)


def gated_deltanet_bidir(q, k, v, beta_logit, decay_logit, conv_w, a_log, dt_bias):
    b, s, _ = q.shape
    qkv = jax.nn.silu(centred_depthwise_conv(jnp.concatenate([q, k, v], axis=-1), conv_w))
    q, k, v = jnp.split(qkv.astype(jnp.float32), 3, axis=-1)
    q = l2_norm(q.reshape(b, s, DN_HEADS, DN_HEAD_DIM)) * (DN_HEAD_DIM ** -0.5)
    k = l2_norm(k.reshape(b, s, DN_HEADS, DN_HEAD_DIM))
    v = v.reshape(b, s, DN_HEADS, DN_HEAD_DIM)
    beta = jax.nn.sigmoid(beta_logit.astype(jnp.float32).reshape(b, s, N_DIR, DN_HEADS))
    dl = decay_logit.astype(jnp.float32).reshape(b, s, N_DIR, DN_HEADS)
    g = -jnp.exp(a_log.astype(jnp.float32)) * jax.nn.softplus(dl + dt_bias.astype(jnp.float32))
    fwd = gated_delta_rule_chunked(q, k, v, g[:, :, 0], beta[:, :, 0])
    flip = lambda a: jnp.flip(a, axis=1)
    bwd = flip(gated_delta_rule_chunked(flip(q), flip(k), flip(v), flip(g[:, :, 1]), flip(beta[:, :, 1])))
    return fwd + bwd


def windowed_gqa_with_sinks(q, k, v, q_norm_w, k_norm_w, sinks):
    b, s, _ = q.shape
    nb = s // BLOCK
    nw = WINDOW // BLOCK
    span = BLOCK + 2 * WINDOW
    grp = SWA_Q_HEADS // SWA_KV_HEADS
    cos, sin = rope_tables(s, SWA_HEAD_DIM)
    q = rotary(rms_norm(q.astype(jnp.float32).reshape(b, s, SWA_Q_HEADS, SWA_HEAD_DIM), q_norm_w), cos, sin)
    k = rotary(rms_norm(k.astype(jnp.float32).reshape(b, s, SWA_KV_HEADS, SWA_HEAD_DIM), k_norm_w), cos, sin)
    v = v.astype(jnp.float32).reshape(b, s, SWA_KV_HEADS, SWA_HEAD_DIM)
    qb = q.reshape(b, nb, BLOCK, SWA_KV_HEADS, grp, SWA_HEAD_DIM)

    def band(a):
        ap = jnp.pad(a, ((0, 0), (WINDOW, WINDOW), (0, 0), (0, 0)))
        ab = ap.reshape(b, nb + 2 * nw, BLOCK, SWA_KV_HEADS, SWA_HEAD_DIM)
        return jnp.concatenate([ab[:, i:i + nb] for i in range(2 * nw + 1)], axis=2)

    kb, vb = band(k), band(v)
    scores = jnp.einsum('bnqhgd,bnkhd->bnhgqk', qb, kb) * (SWA_HEAD_DIM ** -0.5)
    q_pos = jnp.arange(nb)[:, None] * BLOCK + jnp.arange(BLOCK)[None, :]
    k_pos = jnp.arange(nb)[:, None] * BLOCK - WINDOW + jnp.arange(span)[None, :]
    rel = k_pos[:, None, :] - q_pos[:, :, None]
    valid = (jnp.abs(rel) <= WINDOW) & (k_pos[:, None, :] >= 0) & (k_pos[:, None, :] < s)
    scores = jnp.where(valid[None, :, None, None], scores, -jnp.inf)
    sink = sinks.astype(jnp.float32).reshape(SWA_KV_HEADS, grp)[None, None, :, :, None, None]
    mx = jnp.maximum(jnp.max(scores, axis=-1, keepdims=True), sink)
    p = jnp.exp(scores - mx)
    denom = jnp.sum(p, axis=-1, keepdims=True) + jnp.exp(sink - mx)
    out = jnp.einsum('bnhgqk,bnkhd->bnqhgd', p / denom, vb)
    return out.reshape(b, s, SWA_WIDTH)


def setup_inputs(seed: int = 0) -> dict:
    key = jax.random.key(seed)
    ks = jax.random.split(key, 12)
    x = jax.random.normal(ks[0], (BATCH, SEQ, D_MODEL), jnp.float32)
    norm_w = 1.0 + 0.02 * jax.random.normal(ks[1], (DEPTH, D_MODEL), jnp.float32)
    w_in = jax.random.normal(ks[2], (DEPTH, D_MODEL, IN_WIDTH), jnp.float32) * D_MODEL ** -0.5
    dn_conv_w = jax.random.normal(ks[3], (DEPTH, DN_CONV, 3 * DN_WIDTH), jnp.float32) * DN_CONV ** -0.5
    dn_a_log = jnp.log(jax.random.uniform(ks[4], (DEPTH, N_DIR, DN_HEADS), jnp.float32, 1.0, 16.0))
    dt = jnp.exp(jax.random.uniform(ks[5], (DEPTH, N_DIR, DN_HEADS), jnp.float32,
                                    float(np.log(1e-3)), float(np.log(1e-1))))
    dn_dt_bias = dt + jnp.log(-jnp.expm1(-dt))
    dn_out_norm_w = 1.0 + 0.02 * jax.random.normal(ks[6], (DEPTH, DN_HEAD_DIM), jnp.float32)
    swa_q_norm_w = 1.0 + 0.02 * jax.random.normal(ks[7], (DEPTH, SWA_HEAD_DIM), jnp.float32)
    swa_k_norm_w = 1.0 + 0.02 * jax.random.normal(ks[8], (DEPTH, SWA_HEAD_DIM), jnp.float32)
    swa_sinks = jax.random.normal(ks[9], (DEPTH, SWA_Q_HEADS), jnp.float32)
    w_out = jax.random.normal(ks[10], (DEPTH, MIX_WIDTH, D_MODEL), jnp.float32) * MIX_WIDTH ** -0.5
    return {'x': x, 'norm_w': norm_w, 'w_in': w_in, 'dn_conv_w': dn_conv_w, 'dn_a_log': dn_a_log,
            'dn_dt_bias': dn_dt_bias, 'dn_out_norm_w': dn_out_norm_w, 'swa_q_norm_w': swa_q_norm_w,
            'swa_k_norm_w': swa_k_norm_w, 'swa_sinks': swa_sinks, 'w_out': w_out}


def reference(x, norm_w, w_in, dn_conv_w, dn_a_log, dn_dt_bias, dn_out_norm_w,
              swa_q_norm_w, swa_k_norm_w, swa_sinks, w_out):
    b, s, _ = x.shape
    cuts = []
    acc = 0
    for width in SPLITS[:-1]:
        acc += width
        cuts.append(acc)
    for l in range(DEPTH):
        h = rms_norm(x, norm_w[l])
        proj = h @ w_in[l].astype(h.dtype)
        (dn_q, dn_k, dn_v, dn_z, dn_beta, dn_decay,
         sw_q, sw_k, sw_v, sw_z) = jnp.split(proj, cuts, axis=-1)
        dn = gated_deltanet_bidir(dn_q, dn_k, dn_v, dn_beta, dn_decay,
                                  dn_conv_w[l], dn_a_log[l], dn_dt_bias[l])
        dn = rms_norm(dn, dn_out_norm_w[l]).reshape(b, s, DN_WIDTH)
        dn = dn * jax.nn.silu(dn_z.astype(jnp.float32))
        sw = windowed_gqa_with_sinks(sw_q, sw_k, sw_v, swa_q_norm_w[l], swa_k_norm_w[l], swa_sinks[l])
        sw = sw * jax.nn.silu(sw_z.astype(jnp.float32))
        mix = jnp.concatenate([dn, sw], axis=-1).astype(x.dtype)
        x = x + mix @ w_out[l].astype(x.dtype)
    return x
```

```python
import functools

import jax
import jax.numpy as jnp
from jax import lax
from jax.experimental import pallas as pl
from jax.experimental.pallas import tpu as pltpu

D_MODEL = 1024
DN_HEADS = 4
DN_HEAD_DIM = 128
DN_WIDTH = DN_HEADS * DN_HEAD_DIM
DN_CONV = 5
N_DIR = 2
N_GATE = N_DIR * DN_HEADS
SWA_Q_HEADS = 8
SWA_KV_HEADS = 2
SWA_HEAD_DIM = 64
SWA_WIDTH = SWA_Q_HEADS * SWA_HEAD_DIM
SWA_KV_WIDTH = SWA_KV_HEADS * SWA_HEAD_DIM
WINDOW = 128
ROPE_THETA = 10000.0
EPS = 1e-6

LANES = 128
SUBLANES = 8
TILE = 128
PROJ_ROWS = 256
CONV_ROWS = 256
VMEM_LIMIT = 48 * 1024 * 1024

COL_QKV = 0
COL_DNZ = 3 * DN_WIDTH
COL_SWQ = COL_DNZ + DN_WIDTH
COL_SWZ = COL_SWQ + SWA_WIDTH
COL_SWK = COL_SWZ + SWA_WIDTH
COL_SWV = COL_SWK + SWA_KV_WIDTH
COL_GATE = COL_SWV + SWA_KV_WIDTH
PROJ_WIDTH = COL_GATE + LANES

F32 = jnp.float32
BF16 = jnp.bfloat16
NT_DIMS = (((1,), (1,)), ((), ()))
TN_DIMS = (((0,), (0,)), ((), ()))


def _silu(x):
    return x * jax.nn.sigmoid(x)


def _softplus(x):
    return jnp.maximum(x, 0.0) + jnp.log1p(jnp.exp(-jnp.abs(x)))


def _dot(a, b):
    return jnp.dot(a.astype(BF16), b.astype(BF16), preferred_element_type=F32)


def _dot_nt(a, b):
    return lax.dot_general(a.astype(BF16), b.astype(BF16), NT_DIMS, preferred_element_type=F32)


def _dot_tn(a, b):
    return lax.dot_general(a.astype(BF16), b.astype(BF16), TN_DIMS, preferred_element_type=F32)


def _proj_kernel(x_ref, nw_ref, w_ref, wgt_ref, proj_ref, gt_ref):
    x = x_ref[...]
    ms = jnp.mean(x * x, axis=-1, keepdims=True)
    h = ((x * lax.rsqrt(ms + EPS)) * nw_ref[...]).astype(BF16)
    proj_ref[...] = jnp.dot(h, w_ref[...], preferred_element_type=F32)
    gt_ref[...] = lax.dot_general(wgt_ref[...], h, NT_DIMS, preferred_element_type=F32)


def _in_projection(x2d, norm_w, w_perm, w_gate_t):
    rows = x2d.shape[0]
    return pl.pallas_call(
        _proj_kernel,
        grid=(rows // PROJ_ROWS,),
        in_specs=[
            pl.BlockSpec((PROJ_ROWS, D_MODEL), lambda i: (i, 0)),
            pl.BlockSpec((1, D_MODEL), lambda i: (0, 0)),
            pl.BlockSpec((D_MODEL, PROJ_WIDTH), lambda i: (0, 0)),
            pl.BlockSpec((2 * N_GATE, D_MODEL), lambda i: (0, 0)),
        ],
        out_specs=[
            pl.BlockSpec((PROJ_ROWS, PROJ_WIDTH), lambda i: (i, 0)),
            pl.BlockSpec((2 * N_GATE, PROJ_ROWS), lambda i: (0, i)),
        ],
        out_shape=[
            jax.ShapeDtypeStruct((rows, PROJ_WIDTH), F32),
            jax.ShapeDtypeStruct((2 * N_GATE, rows), F32),
        ],
        compiler_params=pltpu.CompilerParams(
            dimension_semantics=("arbitrary",), vmem_limit_bytes=VMEM_LIMIT),
        name="in_proj",
    )(x2d, norm_w, w_perm, w_gate_t)


def _conv_kernel(cur_ref, prev_ref, next_ref, cw_ref, q_ref, k_ref, v_ref, buf_ref):
    i = pl.program_id(1)
    n = pl.num_programs(1)
    rows = cur_ref.shape[1]
    pad = (DN_CONV - 1) // 2
    buf_ref[SUBLANES:SUBLANES + rows, :] = cur_ref[0]
    buf_ref[0:SUBLANES, :] = jnp.where(i > 0, prev_ref[0], 0.0)
    buf_ref[SUBLANES + rows:2 * SUBLANES + rows, :] = jnp.where(i < n - 1, next_ref[0], 0.0)
    outs = (q_ref, k_ref, v_ref)
    for c in range(3 * DN_HEADS):
        lo = c * LANES
        acc = None
        for j in range(DN_CONV):
            r0 = SUBLANES - pad + j
            term = buf_ref[r0:r0 + rows, lo:lo + LANES] * cw_ref[j:j + 1, lo:lo + LANES]
            acc = term if acc is None else acc + term
        y = _silu(acc)
        part, head = divmod(c, DN_HEADS)
        if part < 2:
            y = y * lax.rsqrt(jnp.sum(y * y, axis=-1, keepdims=True) + EPS)
        if part == 0:
            y = y * (DN_HEAD_DIM ** -0.5)
        outs[part][0, :, head * LANES:(head + 1) * LANES] = y


def _dn_conv(proj3d, conv_w):
    b, s, _ = proj3d.shape
    n_tiles = s // CONV_ROWS
    sub_per_tile = CONV_ROWS // SUBLANES
    last_sub = s // SUBLANES - 1
    width = 3 * DN_WIDTH
    out_spec = pl.BlockSpec((1, CONV_ROWS, DN_WIDTH), lambda bb, i: (bb, i, 0))
    out_sds = jax.ShapeDtypeStruct((b, s, DN_WIDTH), F32)
    return pl.pallas_call(
        _conv_kernel,
        grid=(b, n_tiles),
        in_specs=[
            pl.BlockSpec((1, CONV_ROWS, width), lambda bb, i: (bb, i, 0)),
            pl.BlockSpec((1, SUBLANES, width),
                         lambda bb, i: (bb, jnp.maximum(i * sub_per_tile - 1, 0), 0)),
            pl.BlockSpec((1, SUBLANES, width),
                         lambda bb, i: (bb, jnp.minimum((i + 1) * sub_per_tile, last_sub), 0)),
            pl.BlockSpec((SUBLANES, width), lambda bb, i: (0, 0)),
        ],
        out_specs=[out_spec, out_spec, out_spec],
        out_shape=[out_sds, out_sds, out_sds],
        scratch_shapes=[pltpu.VMEM((CONV_ROWS + 2 * SUBLANES, width), F32)],
        compiler_params=pltpu.CompilerParams(
            dimension_semantics=("arbitrary", "arbitrary"), vmem_limit_bytes=VMEM_LIMIT),
        name="dn_conv",
    )(proj3d, proj3d, proj3d, conv_w)


def _split3(x):
    x1 = x.astype(BF16)
    r1 = x - x1.astype(F32)
    x2 = r1.astype(BF16)
    x3 = (r1 - x2.astype(F32)).astype(BF16)
    return x1, x2, x3


def _exact_left(m_bf16, x):
    return sum(jnp.dot(m_bf16, p, preferred_element_type=F32) for p in _split3(x))


def _exact_right(x, m_bf16):
    return sum(jnp.dot(p, m_bf16, preferred_element_type=F32) for p in _split3(x))


def _dot_split(a, b):
    ah = a.astype(BF16)
    al = (a - ah.astype(F32)).astype(BF16)
    bh = b.astype(BF16)
    bl = (b - bh.astype(F32)).astype(BF16)
    return (jnp.dot(ah, bh, preferred_element_type=F32)
            + (jnp.dot(ah, bl, preferred_element_type=F32) + jnp.dot(al, bh, preferred_element_type=F32)))


BASE_BLOCK = 16


def _tri_inverse_minus_eye(x0, row, col):
    same = lambda s: (row // s) == (col // s)
    xk = jnp.where(same(BASE_BLOCK), x0, 0.0)
    p = xk
    n_factors = BASE_BLOCK.bit_length() - 1
    xk = _dot_split(xk, xk)
    for level in range(1, n_factors):
        if level < n_factors - 1:
            res = _dot_split(xk, jnp.concatenate([xk, p], axis=1))
            x_next, xp = res[:, :TILE], res[:, TILE:]
        else:
            x_next, xp = None, _dot_split(xk, p)
        p = p + xk + xp
        xk = x_next
    s = BASE_BLOCK
    while s < TILE:
        m_off = jnp.where(same(2 * s) & jnp.logical_not(same(s)), -x0, 0.0)
        z = m_off + _dot(m_off, p)
        p = p - z - _dot(p, z)
        s *= 2
    return p


def _dn_kernel(qf_ref, kf_ref, vf_ref, gcf_ref, grf_ref,
               qb_ref, kb_ref, vb_ref, gcb_ref, grb_ref,
               alr_ref, dtr_ref, alc_ref, dtc_ref,
               of_ref, ob_ref, s_ref):
    @pl.when(pl.program_id(1) == 0)
    def _():
        s_ref[...] = jnp.zeros_like(s_ref)

    row = lax.broadcasted_iota(jnp.int32, (TILE, TILE), 0)
    col = lax.broadcasted_iota(jnp.int32, (TILE, TILE), 1)
    lower = (col <= row)
    upper = (col >= row)
    diag = (col == row)
    lower_bf = lower.astype(BF16)
    upper_bf = upper.astype(BF16)
    ones_bf = jnp.ones((TILE, TILE), BF16)

    streams = ((qf_ref, kf_ref, vf_ref, gcf_ref, grf_ref, of_ref, lower, lower_bf, upper_bf),
               (qb_ref, kb_ref, vb_ref, gcb_ref, grb_ref, ob_ref, upper, upper_bf, lower_bf))
    for d, (q_ref, k_ref, v_ref, gc_ref, gr_ref, o_ref, incl, cum_left, cum_right) in enumerate(streams):
        gate_c = gc_ref[0]
        gate_r = gr_ref[...]
        beta_c = jax.nn.sigmoid(gate_c)
        g_c = -jnp.exp(alr_ref[...]) * _softplus(gate_c + dtr_ref[...])
        g_r = -jnp.exp(alc_ref[...]) * _softplus(gate_r + dtc_ref[...])
        cum_c = _exact_left(cum_left, g_c)
        tot_c = _exact_left(ones_bf, g_c)
        cum_r = _exact_right(g_r, cum_right)
        for h in range(DN_HEADS):
            cb = d * DN_HEADS + h
            cg = N_GATE + cb
            hs = slice(h * LANES, (h + 1) * LANES)
            q = q_ref[0, :, hs]
            k = k_ref[0, :, hs]
            v = v_ref[0, :, hs]
            beta = beta_c[:, cb:cb + 1]
            gcc = cum_c[:, cg:cg + 1]
            gcr = cum_r[cg:cg + 1, :]
            tot = tot_c[:, cg:cg + 1]
            decay = jnp.where(incl, jnp.exp(jnp.where(incl, gcc - gcr, 0.0)), 0.0)
            gram = _dot_nt(k, k)
            qk = _dot_nt(q, k)
            x0 = jnp.where(diag, 0.0, -(gram * decay * beta))
            p = _tri_inverse_minus_eye(x0, row, col)
            egc = jnp.exp(gcc)
            rhs = jnp.concatenate([v * beta, k * (beta * egc)], axis=1)
            uw = rhs + _dot(p, rhs)
            a = qk * decay
            auw = _dot(a, uw)
            p_q = q * egc - auw[:, TILE:]
            o_intra = auw[:, :TILE]
            kd = k * jnp.exp(tot - gcc)
            kt = _dot_tn(kd, uw)
            state = s_ref[cb]
            res = _dot(jnp.concatenate([kt[:, TILE:], p_q], axis=0), state)
            o_ref[0, :, hs] = res[TILE:] + o_intra
            s_ref[cb] = jnp.exp(tot[0:1, :]) * state - res[:TILE] + kt[:, :TILE]


def _dn_scan(q, k, v, proj3d, gate_t, al_row, dt_row, al_col, dt_col):
    b, s, _ = q.shape
    n_tiles = s // TILE
    gate_blk = COL_GATE // LANES
    fwd = lambda bb, i: (bb, i, 0)
    bwd = lambda bb, i: (bb, n_tiles - 1 - i, 0)
    qkv_f = pl.BlockSpec((1, TILE, DN_WIDTH), fwd)
    qkv_b = pl.BlockSpec((1, TILE, DN_WIDTH), bwd)
    gc_f = pl.BlockSpec((1, TILE, LANES), lambda bb, i: (bb, i, gate_blk))
    gc_b = pl.BlockSpec((1, TILE, LANES), lambda bb, i: (bb, n_tiles - 1 - i, gate_blk))
    gr_f = pl.BlockSpec((2 * N_GATE, TILE), lambda bb, i: (0, bb * n_tiles + i))
    gr_b = pl.BlockSpec((2 * N_GATE, TILE), lambda bb, i: (0, bb * n_tiles + n_tiles - 1 - i))
    row_p = pl.BlockSpec((1, LANES), lambda bb, i: (0, 0))
    col_p = pl.BlockSpec((2 * N_GATE, 1), lambda bb, i: (0, 0))
    out_sds = jax.ShapeDtypeStruct((b, s, DN_WIDTH), F32)
    return pl.pallas_call(
        _dn_kernel,
        grid=(b, n_tiles),
        in_specs=[qkv_f, qkv_f, qkv_f, gc_f, gr_f,
                  qkv_b, qkv_b, qkv_b, gc_b, gr_b,
                  row_p, row_p, col_p, col_p],
        out_specs=[qkv_f, qkv_b],
        out_shape=[out_sds, out_sds],
        scratch_shapes=[pltpu.VMEM((N_GATE, DN_HEAD_DIM, DN_HEAD_DIM), F32)],
        compiler_params=pltpu.CompilerParams(
            dimension_semantics=("arbitrary", "arbitrary"), vmem_limit_bytes=VMEM_LIMIT),
        name="dn_scan",
    )(q, k, v, proj3d, gate_t, q, k, v, proj3d, gate_t, al_row, dt_row, al_col, dt_col)


def _pair_rms(x, w, lo_half):
    x2 = x * x
    s_lo = jnp.sum(jnp.where(lo_half, x2, 0.0), axis=-1, keepdims=True)
    s_hi = jnp.sum(jnp.where(lo_half, 0.0, x2), axis=-1, keepdims=True)
    ms = jnp.where(lo_half, s_lo, s_hi) * (1.0 / SWA_HEAD_DIM)
    return (x * lax.rsqrt(ms + EPS)) * w


def _pair_rope(x, cos, sin_signed, first_quarter):
    half = SWA_HEAD_DIM // 2
    rot = jnp.where(first_quarter, pltpu.roll(x, LANES - half, 1), pltpu.roll(x, half, 1))
    return x * cos + rot * sin_signed


def _swa_kernel(q_ref, kp_ref, kc_ref, kn_ref, vp_ref, vc_ref, vn_ref,
                cp_ref, sp_ref, cc_ref, sc_ref, cn_ref, sn_ref,
                qw_ref, kw_ref, sink_ref, o_ref):
    i = pl.program_id(1)
    n = pl.num_programs(1)
    lane = lax.broadcasted_iota(jnp.int32, (TILE, LANES), 1)
    lo_half = (lane % LANES) < SWA_HEAD_DIM
    first_quarter = (lane % SWA_HEAD_DIM) < (SWA_HEAD_DIM // 2)

    def prep(x, w, cos_ref, sin_ref):
        return _pair_rope(_pair_rms(x, w, lo_half), cos_ref[...], sin_ref[...], first_quarter)

    kw = kw_ref[...]
    k_span = jnp.concatenate([prep(kp_ref[0], kw, cp_ref, sp_ref),
                              prep(kc_ref[0], kw, cc_ref, sc_ref),
                              prep(kn_ref[0], kw, cn_ref, sn_ref)], axis=0)
    v_span = jnp.concatenate([vp_ref[0], vc_ref[0], vn_ref[0]], axis=0)
    span = 3 * TILE
    r = lax.broadcasted_iota(jnp.int32, (TILE, span), 0)
    c = lax.broadcasted_iota(jnp.int32, (TILE, span), 1)
    rel = c - WINDOW - r
    valid = (jnp.abs(rel) <= WINDOW)
    valid = valid & ((c >= TILE) | (i > 0)) & ((c < 2 * TILE) | (i < n - 1))
    group = SWA_Q_HEADS // SWA_KV_HEADS
    qw = qw_ref[...]
    for j in range(SWA_Q_HEADS // 2):
        q_pair = prep(q_ref[0, :, j * LANES:(j + 1) * LANES], qw, cc_ref, sc_ref)
        for half in range(2):
            hq = 2 * j + half
            g = hq // group
            q_h = q_pair[:, half * SWA_HEAD_DIM:(half + 1) * SWA_HEAD_DIM]
            k_g = k_span[:, g * SWA_HEAD_DIM:(g + 1) * SWA_HEAD_DIM]
            v_g = v_span[:, g * SWA_HEAD_DIM:(g + 1) * SWA_HEAD_DIM]
            sc = _dot_nt(q_h, k_g) * (SWA_HEAD_DIM ** -0.5)
            sc = jnp.where(valid, sc, -1e30)
            sink = sink_ref[hq]
            mx = jnp.maximum(jnp.max(sc, axis=-1, keepdims=True), sink)
            p = jnp.exp(sc - mx)
            denom = jnp.sum(p, axis=-1, keepdims=True) + jnp.exp(sink - mx)
            o_ref[0, :, hq * SWA_HEAD_DIM:(hq + 1) * SWA_HEAD_DIM] = _dot(p / denom, v_g)


def _swa(proj3d, cos_t, sin_t, qw, kw, sinks):
    b, s, _ = proj3d.shape
    n_tiles = s // TILE
    kblk = COL_SWK // LANES
    vblk = COL_SWV // LANES
    prev = lambda i: jnp.maximum(i - 1, 0)
    nxt = lambda i: jnp.minimum(i + 1, n_tiles - 1)

    def kv_spec(blk, f):
        return pl.BlockSpec((1, TILE, LANES), lambda bb, i: (bb, f(i), blk))

    def tab_spec(f):
        return pl.BlockSpec((TILE, LANES), lambda bb, i: (f(i), 0))

    ident = lambda i: i
    row_p = pl.BlockSpec((1, LANES), lambda bb, i: (0, 0))
    return pl.pallas_call(
        _swa_kernel,
        grid=(b, n_tiles),
        in_specs=[
            pl.BlockSpec((1, TILE, SWA_WIDTH), lambda bb, i: (bb, i, COL_SWQ // SWA_WIDTH)),
            kv_spec(kblk, prev), kv_spec(kblk, ident), kv_spec(kblk, nxt),
            kv_spec(vblk, prev), kv_spec(vblk, ident), kv_spec(vblk, nxt),
            tab_spec(prev), tab_spec(prev), tab_spec(ident), tab_spec(ident), tab_spec(nxt), tab_spec(nxt),
            row_p, row_p,
            pl.BlockSpec(memory_space=pltpu.SMEM),
        ],
        out_specs=pl.BlockSpec((1, TILE, SWA_WIDTH), lambda bb, i: (bb, i, 0)),
        out_shape=jax.ShapeDtypeStruct((b, s, SWA_WIDTH), F32),
        compiler_params=pltpu.CompilerParams(
            dimension_semantics=("arbitrary", "arbitrary"), vmem_limit_bytes=VMEM_LIMIT),
        name="swa",
    )(proj3d, proj3d, proj3d, proj3d, proj3d, proj3d, proj3d,
      cos_t, sin_t, cos_t, sin_t, cos_t, sin_t, qw, kw, sinks)


def _out_kernel(of_ref, ob_ref, z_ref, sw_ref, swz_ref, x_ref, onw_ref, wo_ref, y_ref):
    dn = of_ref[...] + ob_ref[...]
    onw = onw_ref[...]
    parts = []
    for h in range(DN_HEADS):
        blk = dn[:, h * LANES:(h + 1) * LANES]
        ms = jnp.mean(blk * blk, axis=-1, keepdims=True)
        parts.append((blk * lax.rsqrt(ms + EPS)) * onw)
    dn_n = jnp.concatenate(parts, axis=1) * _silu(z_ref[...])
    sw = sw_ref[...] * _silu(swz_ref[...])
    mix = jnp.concatenate([dn_n, sw], axis=1).astype(BF16)
    y_ref[...] = x_ref[...] + jnp.dot(mix, wo_ref[...], preferred_element_type=F32)


def _out_projection(o_f, o_b, proj2d, sw, x2d, out_norm_w, w_out):
    rows = x2d.shape[0]
    half = lambda i: (i, 0)
    return pl.pallas_call(
        _out_kernel,
        grid=(rows // PROJ_ROWS,),
        in_specs=[
            pl.BlockSpec((PROJ_ROWS, DN_WIDTH), half),
            pl.BlockSpec((PROJ_ROWS, DN_WIDTH), half),
            pl.BlockSpec((PROJ_ROWS, DN_WIDTH), lambda i: (i, COL_DNZ // DN_WIDTH)),
            pl.BlockSpec((PROJ_ROWS, SWA_WIDTH), half),
            pl.BlockSpec((PROJ_ROWS, SWA_WIDTH), lambda i: (i, COL_SWZ // SWA_WIDTH)),
            pl.BlockSpec((PROJ_ROWS, D_MODEL), half),
            pl.BlockSpec((1, LANES), lambda i: (0, 0)),
            pl.BlockSpec((DN_WIDTH + SWA_WIDTH, D_MODEL), lambda i: (0, 0)),
        ],
        out_specs=pl.BlockSpec((PROJ_ROWS, D_MODEL), half),
        out_shape=jax.ShapeDtypeStruct((rows, D_MODEL), F32),
        compiler_params=pltpu.CompilerParams(
            dimension_semantics=("arbitrary",), vmem_limit_bytes=VMEM_LIMIT),
        name="out_proj",
    )(o_f, o_b, proj2d, sw, proj2d, x2d, out_norm_w, w_out)


def _rope_tables(seq):
    inv_freq = ROPE_THETA ** (-jnp.arange(0, SWA_HEAD_DIM, 2, dtype=F32) / SWA_HEAD_DIM)
    ang = jnp.arange(seq, dtype=F32)[:, None] * inv_freq[None, :]
    ang = jnp.concatenate([ang, ang], axis=-1)
    cos, sin = jnp.cos(ang), jnp.sin(ang)
    half = SWA_HEAD_DIM // 2
    sin_signed = jnp.concatenate([-sin[:, :half], sin[:, half:]], axis=-1)
    reps = LANES // SWA_HEAD_DIM
    return jnp.tile(cos, (1, reps)), jnp.tile(sin_signed, (1, reps))


def _layer(x, norm_w, w_in, conv_w, a_log, dt_bias, out_norm_w, q_norm_w, k_norm_w, sinks, w_out):
    b, s, _ = x.shape
    x2d = x.reshape(b * s, D_MODEL)
    o_q, o_k, o_v, o_z = 0, DN_WIDTH, 2 * DN_WIDTH, 3 * DN_WIDTH
    o_beta = 4 * DN_WIDTH
    o_decay = o_beta + N_GATE
    o_swq = o_decay + N_GATE
    o_swk = o_swq + SWA_WIDTH
    o_swv = o_swk + SWA_KV_WIDTH
    o_swz = o_swv + SWA_KV_WIDTH
    w_gate = w_in[:, o_beta:o_swq]
    w_perm = jnp.concatenate([
        w_in[:, o_q:o_beta],
        w_in[:, o_swq:o_swk], w_in[:, o_swz:o_swz + SWA_WIDTH],
        w_in[:, o_swk:o_swv], w_in[:, o_swv:o_swz],
        w_gate, jnp.zeros((D_MODEL, LANES - 2 * N_GATE), w_in.dtype)], axis=1).astype(BF16)
    proj2d, gate_t = _in_projection(x2d, norm_w.reshape(1, D_MODEL), w_perm, w_gate.T.astype(BF16))
    proj3d = proj2d.reshape(b, s, PROJ_WIDTH)

    conv_pad = jnp.concatenate([conv_w, jnp.zeros((SUBLANES - DN_CONV, 3 * DN_WIDTH), conv_w.dtype)], axis=0)
    q, k, v = _dn_conv(proj3d, conv_pad)

    a_flat = a_log.reshape(N_GATE).astype(F32)
    d_flat = dt_bias.reshape(N_GATE).astype(F32)
    zeros8 = jnp.zeros((N_GATE,), F32)
    pad_row = jnp.zeros((LANES - 2 * N_GATE,), F32)
    al_row = jnp.concatenate([zeros8, a_flat, pad_row]).reshape(1, LANES)
    dt_row = jnp.concatenate([zeros8, d_flat, pad_row]).reshape(1, LANES)
    al_col = jnp.concatenate([zeros8, a_flat]).reshape(2 * N_GATE, 1)
    dt_col = jnp.concatenate([zeros8, d_flat]).reshape(2 * N_GATE, 1)
    o_f, o_b = _dn_scan(q, k, v, proj3d, gate_t, al_row, dt_row, al_col, dt_col)

    cos_t, sin_t = _rope_tables(s)
    reps = LANES // SWA_HEAD_DIM
    sw = _swa(proj3d, cos_t, sin_t,
              jnp.tile(q_norm_w.astype(F32), reps).reshape(1, LANES),
              jnp.tile(k_norm_w.astype(F32), reps).reshape(1, LANES),
              sinks.astype(F32))

    y = _out_projection(o_f.reshape(b * s, DN_WIDTH), o_b.reshape(b * s, DN_WIDTH), proj2d,
                        sw.reshape(b * s, SWA_WIDTH), x2d,
                        out_norm_w.reshape(1, LANES).astype(F32), w_out.astype(BF16))
    return y.reshape(b, s, D_MODEL)


def kernel(x, norm_w, w_in, dn_conv_w, dn_a_log, dn_dt_bias, dn_out_norm_w,
           swa_q_norm_w, swa_k_norm_w, swa_sinks, w_out):
    for l in range(norm_w.shape[0]):
        x = _layer(x, norm_w[l], w_in[l], dn_conv_w[l], dn_a_log[l], dn_dt_bias[l], dn_out_norm_w[l],
                   swa_q_norm_w[l], swa_k_norm_w[l], swa_sinks[l], w_out[l])
    return x
```

```python
import functools

import jax
import jax.numpy as jnp
from jax import lax
from jax.experimental import pallas as pl
from jax.experimental.pallas import tpu as pltpu

D_MODEL = 1024
DN_HEADS = 4
DN_HEAD_DIM = 128
DN_WIDTH = DN_HEADS * DN_HEAD_DIM
DN_CONV = 5
N_DIR = 2
N_GATE = N_DIR * DN_HEADS
SWA_Q_HEADS = 8
SWA_KV_HEADS = 2
SWA_HEAD_DIM = 64
SWA_WIDTH = SWA_Q_HEADS * SWA_HEAD_DIM
SWA_KV_WIDTH = SWA_KV_HEADS * SWA_HEAD_DIM
WINDOW = 128
ROPE_THETA = 10000.0
EPS = 1e-6

LANES = 128
SUBLANES = 8
TILE = 128
PROJ_ROWS = 256
CONV_ROWS = 256
VMEM_LIMIT = 48 * 1024 * 1024

COL_QKV = 0
COL_DNZ = 3 * DN_WIDTH
COL_SWQ = COL_DNZ + DN_WIDTH
COL_SWZ = COL_SWQ + SWA_WIDTH
COL_SWK = COL_SWZ + SWA_WIDTH
COL_SWV = COL_SWK + SWA_KV_WIDTH
COL_GATE = COL_SWV + SWA_KV_WIDTH
PROJ_WIDTH = COL_GATE + LANES

F32 = jnp.float32
BF16 = jnp.bfloat16
NT_DIMS = (((1,), (1,)), ((), ()))
TN_DIMS = (((0,), (0,)), ((), ()))


def _silu(x):
    return x * jax.nn.sigmoid(x)


def _softplus(x):
    return jnp.maximum(x, 0.0) + jnp.log1p(jnp.exp(-jnp.abs(x)))


def _dot(a, b):
    return jnp.dot(a.astype(BF16), b.astype(BF16), preferred_element_type=F32)


def _dot_nt(a, b):
    return lax.dot_general(a.astype(BF16), b.astype(BF16), NT_DIMS, preferred_element_type=F32)


def _dot_tn(a, b):
    return lax.dot_general(a.astype(BF16), b.astype(BF16), TN_DIMS, preferred_element_type=F32)


def _proj_kernel(x_ref, nw_ref, w_ref, wgt_ref, proj_ref, gt_ref):
    x = x_ref[...]
    ms = jnp.mean(x * x, axis=-1, keepdims=True)
    h = ((x * lax.rsqrt(ms + EPS)) * nw_ref[...]).astype(BF16)
    proj_ref[...] = jnp.dot(h, w_ref[...], preferred_element_type=F32)
    gt_ref[...] = lax.dot_general(wgt_ref[...], h, NT_DIMS, preferred_element_type=F32)


def _in_projection(x2d, norm_w, w_perm, w_gate_t):
    rows = x2d.shape[0]
    return pl.pallas_call(
        _proj_kernel,
        grid=(rows // PROJ_ROWS,),
        in_specs=[
            pl.BlockSpec((PROJ_ROWS, D_MODEL), lambda i: (i, 0)),
            pl.BlockSpec((1, D_MODEL), lambda i: (0, 0)),
            pl.BlockSpec((D_MODEL, PROJ_WIDTH), lambda i: (0, 0)),
            pl.BlockSpec((2 * N_GATE, D_MODEL), lambda i: (0, 0)),
        ],
        out_specs=[
            pl.BlockSpec((PROJ_ROWS, PROJ_WIDTH), lambda i: (i, 0)),
            pl.BlockSpec((2 * N_GATE, PROJ_ROWS), lambda i: (0, i)),
        ],
        out_shape=[
            jax.ShapeDtypeStruct((rows, PROJ_WIDTH), F32),
            jax.ShapeDtypeStruct((2 * N_GATE, rows), F32),
        ],
        compiler_params=pltpu.CompilerParams(
            dimension_semantics=("arbitrary",), vmem_limit_bytes=VMEM_LIMIT),
        name="in_proj",
    )(x2d, norm_w, w_perm, w_gate_t)


def _conv_kernel(cur_ref, prev_ref, next_ref, cw_ref, q_ref, k_ref, v_ref, buf_ref):
    i = pl.program_id(1)
    n = pl.num_programs(1)
    rows = cur_ref.shape[1]
    pad = (DN_CONV - 1) // 2
    buf_ref[SUBLANES:SUBLANES + rows, :] = cur_ref[0]
    buf_ref[0:SUBLANES, :] = jnp.where(i > 0, prev_ref[0], 0.0)
    buf_ref[SUBLANES + rows:2 * SUBLANES + rows, :] = jnp.where(i < n - 1, next_ref[0], 0.0)
    outs = (q_ref, k_ref, v_ref)
    for c in range(3 * DN_HEADS):
        lo = c * LANES
        acc = None
        for j in range(DN_CONV):
            r0 = SUBLANES - pad + j
            term = buf_ref[r0:r0 + rows, lo:lo + LANES] * cw_ref[j:j + 1, lo:lo + LANES]
            acc = term if acc is None else acc + term
        y = _silu(acc)
        part, head = divmod(c, DN_HEADS)
        if part < 2:
            y = y * lax.rsqrt(jnp.sum(y * y, axis=-1, keepdims=True) + EPS)
        if part == 0:
            y = y * (DN_HEAD_DIM ** -0.5)
        outs[part][0, :, head * LANES:(head + 1) * LANES] = y


def _dn_conv(proj3d, conv_w):
    b, s, _ = proj3d.shape
    n_tiles = s // CONV_ROWS
    sub_per_tile = CONV_ROWS // SUBLANES
    last_sub = s // SUBLANES - 1
    width = 3 * DN_WIDTH
    out_spec = pl.BlockSpec((1, CONV_ROWS, DN_WIDTH), lambda bb, i: (bb, i, 0))
    out_sds = jax.ShapeDtypeStruct((b, s, DN_WIDTH), F32)
    return pl.pallas_call(
        _conv_kernel,
        grid=(b, n_tiles),
        in_specs=[
            pl.BlockSpec((1, CONV_ROWS, width), lambda bb, i: (bb, i, 0)),
            pl.BlockSpec((1, SUBLANES, width),
                         lambda bb, i: (bb, jnp.maximum(i * sub_per_tile - 1, 0), 0)),
            pl.BlockSpec((1, SUBLANES, width),
                         lambda bb, i: (bb, jnp.minimum((i + 1) * sub_per_tile, last_sub), 0)),
            pl.BlockSpec((SUBLANES, width), lambda bb, i: (0, 0)),
        ],
        out_specs=[out_spec, out_spec, out_spec],
        out_shape=[out_sds, out_sds, out_sds],
        scratch_shapes=[pltpu.VMEM((CONV_ROWS + 2 * SUBLANES, width), F32)],
        compiler_params=pltpu.CompilerParams(
            dimension_semantics=("arbitrary", "arbitrary"), vmem_limit_bytes=VMEM_LIMIT),
        name="dn_conv",
    )(proj3d, proj3d, proj3d, conv_w)


def _split3(x):
    x1 = x.astype(BF16)
    r1 = x - x1.astype(F32)
    x2 = r1.astype(BF16)
    x3 = (r1 - x2.astype(F32)).astype(BF16)
    return x1, x2, x3


def _exact_left(m_bf16, x):
    return sum(jnp.dot(m_bf16, p, preferred_element_type=F32) for p in _split3(x))


def _exact_right(x, m_bf16):
    return sum(jnp.dot(p, m_bf16, preferred_element_type=F32) for p in _split3(x))


def _dot_split(a, b):
    ah = a.astype(BF16)
    al = (a - ah.astype(F32)).astype(BF16)
    bh = b.astype(BF16)
    bl = (b - bh.astype(F32)).astype(BF16)
    return (jnp.dot(ah, bh, preferred_element_type=F32)
            + (jnp.dot(ah, bl, preferred_element_type=F32) + jnp.dot(al, bh, preferred_element_type=F32)))


BASE_BLOCK = 16


def _tri_inverse_minus_eye(x0, row, col):
    same = lambda s: (row // s) == (col // s)
    xk = jnp.where(same(BASE_BLOCK), x0, 0.0)
    p = xk
    n_factors = BASE_BLOCK.bit_length() - 1
    xk = _dot_split(xk, xk)
    yield
    for level in range(1, n_factors):
        if level < n_factors - 1:
            res = _dot_split(xk, jnp.concatenate([xk, p], axis=1))
            x_next, xp = res[:, :TILE], res[:, TILE:]
        else:
            x_next, xp = None, _dot_split(xk, p)
        p = p + xk + xp
        xk = x_next
        yield
    s = BASE_BLOCK
    while s < TILE:
        m_off = jnp.where(same(2 * s) & jnp.logical_not(same(s)), -x0, 0.0)
        z = m_off + _dot(m_off, p)
        yield
        p = p - z - _dot(p, z)
        yield
        s *= 2
    return p


def _dn_chain(q, k, v, beta, gcc, gcr, tot, incl, diag, row, col, state_ref, o_ref, hs):
    decay = jnp.where(incl, jnp.exp(jnp.where(incl, gcc - gcr, 0.0)), 0.0)
    gram = _dot_nt(k, k)
    qk = _dot_nt(q, k)
    yield
    x0 = jnp.where(diag, 0.0, -(gram * decay * beta))
    p = yield from _tri_inverse_minus_eye(x0, row, col)
    egc = jnp.exp(gcc)
    rhs = jnp.concatenate([v * beta, k * (beta * egc)], axis=1)
    uw = rhs + _dot(p, rhs)
    yield
    auw = _dot(qk * decay, uw)
    kt = _dot_tn(k * jnp.exp(tot - gcc), uw)
    yield
    p_q = q * egc - auw[:, TILE:]
    state = state_ref[...]
    res = _dot(jnp.concatenate([kt[:, TILE:], p_q], axis=0), state)
    yield
    o_ref[0, :, hs] = res[TILE:] + auw[:, :TILE]
    state_ref[...] = jnp.exp(tot[0:1, :]) * state - res[:TILE] + kt[:, :TILE]


def _dn_kernel(qf_ref, kf_ref, vf_ref, gcf_ref, grf_ref,
               qb_ref, kb_ref, vb_ref, gcb_ref, grb_ref,
               alr_ref, dtr_ref, alc_ref, dtc_ref,
               of_ref, ob_ref, s_ref):
    @pl.when(pl.program_id(1) == 0)
    def _():
        s_ref[...] = jnp.zeros_like(s_ref)

    row = lax.broadcasted_iota(jnp.int32, (TILE, TILE), 0)
    col = lax.broadcasted_iota(jnp.int32, (TILE, TILE), 1)
    lower = (col <= row)
    upper = (col >= row)
    diag = (col == row)
    lower_bf = lower.astype(BF16)
    upper_bf = upper.astype(BF16)
    ones_bf = jnp.ones((TILE, TILE), BF16)

    streams = ((qf_ref, kf_ref, vf_ref, gcf_ref, grf_ref, of_ref, lower, lower_bf, upper_bf),
               (qb_ref, kb_ref, vb_ref, gcb_ref, grb_ref, ob_ref, upper, upper_bf, lower_bf))
    chains = []
    for d, (q_ref, k_ref, v_ref, gc_ref, gr_ref, o_ref, incl, cum_left, cum_right) in enumerate(streams):
        gate_c = gc_ref[0]
        gate_r = gr_ref[...]
        beta_c = jax.nn.sigmoid(gate_c)
        g_c = -jnp.exp(alr_ref[...]) * _softplus(gate_c + dtr_ref[...])
        g_r = -jnp.exp(alc_ref[...]) * _softplus(gate_r + dtc_ref[...])
        cum_c = _exact_left(cum_left, g_c)
        tot_c = _exact_left(ones_bf, g_c)
        cum_r = _exact_right(g_r, cum_right)
        for h in range(DN_HEADS):
            cb = d * DN_HEADS + h
            cg = N_GATE + cb
            hs = slice(h * LANES, (h + 1) * LANES)
            chains.append(_dn_chain(
                q_ref[0, :, hs], k_ref[0, :, hs], v_ref[0, :, hs],
                beta_c[:, cb:cb + 1], cum_c[:, cg:cg + 1], cum_r[cg:cg + 1, :], tot_c[:, cg:cg + 1],
                incl, diag, row, col, s_ref.at[cb], o_ref, hs))
    while chains:
        alive = []
        for chain in chains:
            try:
                next(chain)
                alive.append(chain)
            except StopIteration:
                pass
        chains = alive


def _dn_scan(q, k, v, proj3d, gate_t, al_row, dt_row, al_col, dt_col):
    b, s, _ = q.shape
    n_tiles = s // TILE
    gate_blk = COL_GATE // LANES
    fwd = lambda bb, i: (bb, i, 0)
    bwd = lambda bb, i: (bb, n_tiles - 1 - i, 0)
    qkv_f = pl.BlockSpec((1, TILE, DN_WIDTH), fwd)
    qkv_b = pl.BlockSpec((1, TILE, DN_WIDTH), bwd)
    gc_f = pl.BlockSpec((1, TILE, LANES), lambda bb, i: (bb, i, gate_blk))
    gc_b = pl.BlockSpec((1, TILE, LANES), lambda bb, i: (bb, n_tiles - 1 - i, gate_blk))
    gr_f = pl.BlockSpec((2 * N_GATE, TILE), lambda bb, i: (0, bb * n_tiles + i))
    gr_b = pl.BlockSpec((2 * N_GATE, TILE), lambda bb, i: (0, bb * n_tiles + n_tiles - 1 - i))
    row_p = pl.BlockSpec((1, LANES), lambda bb, i: (0, 0))
    col_p = pl.BlockSpec((2 * N_GATE, 1), lambda bb, i: (0, 0))
    out_sds = jax.ShapeDtypeStruct((b, s, DN_WIDTH), F32)
    return pl.pallas_call(
        _dn_kernel,
        grid=(b, n_tiles),
        in_specs=[qkv_f, qkv_f, qkv_f, gc_f, gr_f,
                  qkv_b, qkv_b, qkv_b, gc_b, gr_b,
                  row_p, row_p, col_p, col_p],
        out_specs=[qkv_f, qkv_b],
        out_shape=[out_sds, out_sds],
        scratch_shapes=[pltpu.VMEM((N_GATE, DN_HEAD_DIM, DN_HEAD_DIM), F32)],
        compiler_params=pltpu.CompilerParams(
            dimension_semantics=("arbitrary", "arbitrary"), vmem_limit_bytes=VMEM_LIMIT),
        name="dn_scan",
    )(q, k, v, proj3d, gate_t, q, k, v, proj3d, gate_t, al_row, dt_row, al_col, dt_col)


def _pair_rms(x, w, lo_half):
    x2 = x * x
    s_lo = jnp.sum(jnp.where(lo_half, x2, 0.0), axis=-1, keepdims=True)
    s_hi = jnp.sum(jnp.where(lo_half, 0.0, x2), axis=-1, keepdims=True)
    ms = jnp.where(lo_half, s_lo, s_hi) * (1.0 / SWA_HEAD_DIM)
    return (x * lax.rsqrt(ms + EPS)) * w


def _pair_rope(x, cos, sin_signed, first_quarter):
    half = SWA_HEAD_DIM // 2
    rot = jnp.where(first_quarter, pltpu.roll(x, LANES - half, 1), pltpu.roll(x, half, 1))
    return x * cos + rot * sin_signed


def _swa_kernel(q_ref, kp_ref, kc_ref, kn_ref, vp_ref, vc_ref, vn_ref,
                cp_ref, sp_ref, cc_ref, sc_ref, cn_ref, sn_ref,
                qw_ref, kw_ref, sink_ref, o_ref):
    i = pl.program_id(1)
    n = pl.num_programs(1)
    lane = lax.broadcasted_iota(jnp.int32, (TILE, LANES), 1)
    lo_half = (lane % LANES) < SWA_HEAD_DIM
    first_quarter = (lane % SWA_HEAD_DIM) < (SWA_HEAD_DIM // 2)

    def prep(x, w, cos_ref, sin_ref):
        return _pair_rope(_pair_rms(x, w, lo_half), cos_ref[...], sin_ref[...], first_quarter)

    kw = kw_ref[...]
    k_span = jnp.concatenate([prep(kp_ref[0], kw, cp_ref, sp_ref),
                              prep(kc_ref[0], kw, cc_ref, sc_ref),
                              prep(kn_ref[0], kw, cn_ref, sn_ref)], axis=0)
    v_span = jnp.concatenate([vp_ref[0], vc_ref[0], vn_ref[0]], axis=0)
    span = 3 * TILE
    r = lax.broadcasted_iota(jnp.int32, (TILE, span), 0)
    c = lax.broadcasted_iota(jnp.int32, (TILE, span), 1)
    rel = c - WINDOW - r
    valid = (jnp.abs(rel) <= WINDOW)
    valid = valid & ((c >= TILE) | (i > 0)) & ((c < 2 * TILE) | (i < n - 1))
    group = SWA_Q_HEADS // SWA_KV_HEADS
    qw = qw_ref[...]
    for j in range(SWA_Q_HEADS // 2):
        q_pair = prep(q_ref[0, :, j * LANES:(j + 1) * LANES], qw, cc_ref, sc_ref)
        for half in range(2):
            hq = 2 * j + half
            g = hq // group
            q_h = q_pair[:, half * SWA_HEAD_DIM:(half + 1) * SWA_HEAD_DIM]
            k_g = k_span[:, g * SWA_HEAD_DIM:(g + 1) * SWA_HEAD_DIM]
            v_g = v_span[:, g * SWA_HEAD_DIM:(g + 1) * SWA_HEAD_DIM]
            sc = _dot_nt(q_h, k_g) * (SWA_HEAD_DIM ** -0.5)
            sc = jnp.where(valid, sc, -1e30)
            sink = sink_ref[hq]
            mx = jnp.maximum(jnp.max(sc, axis=-1, keepdims=True), sink)
            p = jnp.exp(sc - mx)
            denom = jnp.sum(p, axis=-1, keepdims=True) + jnp.exp(sink - mx)
            o_ref[0, :, hq * SWA_HEAD_DIM:(hq + 1) * SWA_HEAD_DIM] = _dot(p / denom, v_g)


def _swa(proj3d, cos_t, sin_t, qw, kw, sinks):
    b, s, _ = proj3d.shape
    n_tiles = s // TILE
    kblk = COL_SWK // LANES
    vblk = COL_SWV // LANES
    prev = lambda i: jnp.maximum(i - 1, 0)
    nxt = lambda i: jnp.minimum(i + 1, n_tiles - 1)

    def kv_spec(blk, f):
        return pl.BlockSpec((1, TILE, LANES), lambda bb, i: (bb, f(i), blk))

    def tab_spec(f):
        return pl.BlockSpec((TILE, LANES), lambda bb, i: (f(i), 0))

    ident = lambda i: i
    row_p = pl.BlockSpec((1, LANES), lambda bb, i: (0, 0))
    return pl.pallas_call(
        _swa_kernel,
        grid=(b, n_tiles),
        in_specs=[
            pl.BlockSpec((1, TILE, SWA_WIDTH), lambda bb, i: (bb, i, COL_SWQ // SWA_WIDTH)),
            kv_spec(kblk, prev), kv_spec(kblk, ident), kv_spec(kblk, nxt),
            kv_spec(vblk, prev), kv_spec(vblk, ident), kv_spec(vblk, nxt),
            tab_spec(prev), tab_spec(prev), tab_spec(ident), tab_spec(ident), tab_spec(nxt), tab_spec(nxt),
            row_p, row_p,
            pl.BlockSpec(memory_space=pltpu.SMEM),
        ],
        out_specs=pl.BlockSpec((1, TILE, SWA_WIDTH), lambda bb, i: (bb, i, 0)),
        out_shape=jax.ShapeDtypeStruct((b, s, SWA_WIDTH), F32),
        compiler_params=pltpu.CompilerParams(
            dimension_semantics=("arbitrary", "arbitrary"), vmem_limit_bytes=VMEM_LIMIT),
        name="swa",
    )(proj3d, proj3d, proj3d, proj3d, proj3d, proj3d, proj3d,
      cos_t, sin_t, cos_t, sin_t, cos_t, sin_t, qw, kw, sinks)


def _out_kernel(of_ref, ob_ref, z_ref, sw_ref, swz_ref, x_ref, onw_ref, wo_ref, y_ref):
    dn = of_ref[...] + ob_ref[...]
    onw = onw_ref[...]
    parts = []
    for h in range(DN_HEADS):
        blk = dn[:, h * LANES:(h + 1) * LANES]
        ms = jnp.mean(blk * blk, axis=-1, keepdims=True)
        parts.append((blk * lax.rsqrt(ms + EPS)) * onw)
    dn_n = jnp.concatenate(parts, axis=1) * _silu(z_ref[...])
    sw = sw_ref[...] * _silu(swz_ref[...])
    mix = jnp.concatenate([dn_n, sw], axis=1).astype(BF16)
    y_ref[...] = x_ref[...] + jnp.dot(mix, wo_ref[...], preferred_element_type=F32)


def _out_projection(o_f, o_b, proj2d, sw, x2d, out_norm_w, w_out):
    rows = x2d.shape[0]
    half = lambda i: (i, 0)
    return pl.pallas_call(
        _out_kernel,
        grid=(rows // PROJ_ROWS,),
        in_specs=[
            pl.BlockSpec((PROJ_ROWS, DN_WIDTH), half),
            pl.BlockSpec((PROJ_ROWS, DN_WIDTH), half),
            pl.BlockSpec((PROJ_ROWS, DN_WIDTH), lambda i: (i, COL_DNZ // DN_WIDTH)),
            pl.BlockSpec((PROJ_ROWS, SWA_WIDTH), half),
            pl.BlockSpec((PROJ_ROWS, SWA_WIDTH), lambda i: (i, COL_SWZ // SWA_WIDTH)),
            pl.BlockSpec((PROJ_ROWS, D_MODEL), half),
            pl.BlockSpec((1, LANES), lambda i: (0, 0)),
            pl.BlockSpec((DN_WIDTH + SWA_WIDTH, D_MODEL), lambda i: (0, 0)),
        ],
        out_specs=pl.BlockSpec((PROJ_ROWS, D_MODEL), half),
        out_shape=jax.ShapeDtypeStruct((rows, D_MODEL), F32),
        compiler_params=pltpu.CompilerParams(
            dimension_semantics=("arbitrary",), vmem_limit_bytes=VMEM_LIMIT),
        name="out_proj",
    )(o_f, o_b, proj2d, sw, proj2d, x2d, out_norm_w, w_out)


def _rope_tables(seq):
    inv_freq = ROPE_THETA ** (-jnp.arange(0, SWA_HEAD_DIM, 2, dtype=F32) / SWA_HEAD_DIM)
    ang = jnp.arange(seq, dtype=F32)[:, None] * inv_freq[None, :]
    ang = jnp.concatenate([ang, ang], axis=-1)
    cos, sin = jnp.cos(ang), jnp.sin(ang)
    half = SWA_HEAD_DIM // 2
    sin_signed = jnp.concatenate([-sin[:, :half], sin[:, half:]], axis=-1)
    reps = LANES // SWA_HEAD_DIM
    return jnp.tile(cos, (1, reps)), jnp.tile(sin_signed, (1, reps))


def _layer(x, norm_w, w_in, conv_w, a_log, dt_bias, out_norm_w, q_norm_w, k_norm_w, sinks, w_out):
    b, s, _ = x.shape
    x2d = x.reshape(b * s, D_MODEL)
    o_q, o_k, o_v, o_z = 0, DN_WIDTH, 2 * DN_WIDTH, 3 * DN_WIDTH
    o_beta = 4 * DN_WIDTH
    o_decay = o_beta + N_GATE
    o_swq = o_decay + N_GATE
    o_swk = o_swq + SWA_WIDTH
    o_swv = o_swk + SWA_KV_WIDTH
    o_swz = o_swv + SWA_KV_WIDTH
    w_gate = w_in[:, o_beta:o_swq]
    w_perm = jnp.concatenate([
        w_in[:, o_q:o_beta],
        w_in[:, o_swq:o_swk], w_in[:, o_swz:o_swz + SWA_WIDTH],
        w_in[:, o_swk:o_swv], w_in[:, o_swv:o_swz],
        w_gate, jnp.zeros((D_MODEL, LANES - 2 * N_GATE), w_in.dtype)], axis=1).astype(BF16)
    proj2d, gate_t = _in_projection(x2d, norm_w.reshape(1, D_MODEL), w_perm, w_gate.T.astype(BF16))
    proj3d = proj2d.reshape(b, s, PROJ_WIDTH)

    conv_pad = jnp.concatenate([conv_w, jnp.zeros((SUBLANES - DN_CONV, 3 * DN_WIDTH), conv_w.dtype)], axis=0)
    q, k, v = _dn_conv(proj3d, conv_pad)

    a_flat = a_log.reshape(N_GATE).astype(F32)
    d_flat = dt_bias.reshape(N_GATE).astype(F32)
    zeros8 = jnp.zeros((N_GATE,), F32)
    pad_row = jnp.zeros((LANES - 2 * N_GATE,), F32)
    al_row = jnp.concatenate([zeros8, a_flat, pad_row]).reshape(1, LANES)
    dt_row = jnp.concatenate([zeros8, d_flat, pad_row]).reshape(1, LANES)
    al_col = jnp.concatenate([zeros8, a_flat]).reshape(2 * N_GATE, 1)
    dt_col = jnp.concatenate([zeros8, d_flat]).reshape(2 * N_GATE, 1)
    o_f, o_b = _dn_scan(q, k, v, proj3d, gate_t, al_row, dt_row, al_col, dt_col)

    cos_t, sin_t = _rope_tables(s)
    reps = LANES // SWA_HEAD_DIM
    sw = _swa(proj3d, cos_t, sin_t,
              jnp.tile(q_norm_w.astype(F32), reps).reshape(1, LANES),
              jnp.tile(k_norm_w.astype(F32), reps).reshape(1, LANES),
              sinks.astype(F32))

    y = _out_projection(o_f.reshape(b * s, DN_WIDTH), o_b.reshape(b * s, DN_WIDTH), proj2d,
                        sw.reshape(b * s, SWA_WIDTH), x2d,
                        out_norm_w.reshape(1, LANES).astype(F32), w_out.astype(BF16))
    return y.reshape(b, s, D_MODEL)


def kernel(x, norm_w, w_in, dn_conv_w, dn_a_log, dn_dt_bias, dn_out_norm_w,
           swa_q_norm_w, swa_k_norm_w, swa_sinks, w_out):
    for l in range(norm_w.shape[0]):
        x = _layer(x, norm_w[l], w_in[l], dn_conv_w[l], dn_a_log[l], dn_dt_bias[l], dn_out_norm_w[l],
                   swa_q_norm_w[l], swa_k_norm_w[l], swa_sinks[l], w_out[l])
    return x
```

```python
import functools

import jax
import jax.numpy as jnp
from jax import lax
from jax.experimental import pallas as pl
from jax.experimental.pallas import tpu as pltpu

D_MODEL = 1024
DN_HEADS = 4
DN_HEAD_DIM = 128
DN_WIDTH = DN_HEADS * DN_HEAD_DIM
DN_CONV = 5
N_DIR = 2
N_GATE = N_DIR * DN_HEADS
SWA_Q_HEADS = 8
SWA_KV_HEADS = 2
SWA_HEAD_DIM = 64
SWA_WIDTH = SWA_Q_HEADS * SWA_HEAD_DIM
SWA_KV_WIDTH = SWA_KV_HEADS * SWA_HEAD_DIM
WINDOW = 128
ROPE_THETA = 10000.0
EPS = 1e-6

LANES = 128
SUBLANES = 8
TILE = 128
PROJ_ROWS = 256
CONV_ROWS = 256
VMEM_LIMIT = 48 * 1024 * 1024

COL_QKV = 0
COL_DNZ = 3 * DN_WIDTH
COL_SWQ = COL_DNZ + DN_WIDTH
COL_SWZ = COL_SWQ + SWA_WIDTH
COL_SWK = COL_SWZ + SWA_WIDTH
COL_SWV = COL_SWK + SWA_KV_WIDTH
COL_GATE = COL_SWV + SWA_KV_WIDTH
PROJ_WIDTH = COL_GATE + LANES

F32 = jnp.float32
BF16 = jnp.bfloat16
NT_DIMS = (((1,), (1,)), ((), ()))
TN_DIMS = (((0,), (0,)), ((), ()))


def _silu(x):
    return x * jax.nn.sigmoid(x)


def _softplus(x):
    return jnp.maximum(x, 0.0) + jnp.log1p(jnp.exp(-jnp.abs(x)))


def _dot(a, b):
    return jnp.dot(a.astype(BF16), b.astype(BF16), preferred_element_type=F32)


def _dot_nt(a, b):
    return lax.dot_general(a.astype(BF16), b.astype(BF16), NT_DIMS, preferred_element_type=F32)


def _dot_tn(a, b):
    return lax.dot_general(a.astype(BF16), b.astype(BF16), TN_DIMS, preferred_element_type=F32)


def _proj_kernel(x_ref, nw_ref, w_ref, wgt_ref, proj_ref, gt_ref):
    x = x_ref[...]
    ms = jnp.mean(x * x, axis=-1, keepdims=True)
    h = ((x * lax.rsqrt(ms + EPS)) * nw_ref[...]).astype(BF16)
    proj_ref[...] = jnp.dot(h, w_ref[...], preferred_element_type=F32)
    gt_ref[...] = lax.dot_general(wgt_ref[...], h, NT_DIMS, preferred_element_type=F32)


def _in_projection(x2d, norm_w, w_perm, w_gate_t):
    rows = x2d.shape[0]
    return pl.pallas_call(
        _proj_kernel,
        grid=(rows // PROJ_ROWS,),
        in_specs=[
            pl.BlockSpec((PROJ_ROWS, D_MODEL), lambda i: (i, 0)),
            pl.BlockSpec((1, D_MODEL), lambda i: (0, 0)),
            pl.BlockSpec((D_MODEL, PROJ_WIDTH), lambda i: (0, 0)),
            pl.BlockSpec((2 * N_GATE, D_MODEL), lambda i: (0, 0)),
        ],
        out_specs=[
            pl.BlockSpec((PROJ_ROWS, PROJ_WIDTH), lambda i: (i, 0)),
            pl.BlockSpec((2 * N_GATE, PROJ_ROWS), lambda i: (0, i)),
        ],
        out_shape=[
            jax.ShapeDtypeStruct((rows, PROJ_WIDTH), F32),
            jax.ShapeDtypeStruct((2 * N_GATE, rows), F32),
        ],
        compiler_params=pltpu.CompilerParams(
            dimension_semantics=("arbitrary",), vmem_limit_bytes=VMEM_LIMIT),
        name="in_proj",
    )(x2d, norm_w, w_perm, w_gate_t)


def _conv_kernel(cur_ref, prev_ref, next_ref, cw_ref, q_ref, k_ref, v_ref, buf_ref):
    i = pl.program_id(1)
    n = pl.num_programs(1)
    rows = cur_ref.shape[1]
    pad = (DN_CONV - 1) // 2
    buf_ref[SUBLANES:SUBLANES + rows, :] = cur_ref[0]
    buf_ref[0:SUBLANES, :] = jnp.where(i > 0, prev_ref[0], 0.0)
    buf_ref[SUBLANES + rows:2 * SUBLANES + rows, :] = jnp.where(i < n - 1, next_ref[0], 0.0)
    outs = (q_ref, k_ref, v_ref)
    for c in range(3 * DN_HEADS):
        lo = c * LANES
        acc = None
        for j in range(DN_CONV):
            r0 = SUBLANES - pad + j
            term = buf_ref[r0:r0 + rows, lo:lo + LANES] * cw_ref[j:j + 1, lo:lo + LANES]
            acc = term if acc is None else acc + term
        y = _silu(acc)
        part, head = divmod(c, DN_HEADS)
        if part < 2:
            y = y * lax.rsqrt(jnp.sum(y * y, axis=-1, keepdims=True) + EPS)
        if part == 0:
            y = y * (DN_HEAD_DIM ** -0.5)
        outs[part][0, :, head * LANES:(head + 1) * LANES] = y


def _dn_conv(proj3d, conv_w):
    b, s, _ = proj3d.shape
    n_tiles = s // CONV_ROWS
    sub_per_tile = CONV_ROWS // SUBLANES
    last_sub = s // SUBLANES - 1
    width = 3 * DN_WIDTH
    out_spec = pl.BlockSpec((1, CONV_ROWS, DN_WIDTH), lambda bb, i: (bb, i, 0))
    out_sds = jax.ShapeDtypeStruct((b, s, DN_WIDTH), F32)
    return pl.pallas_call(
        _conv_kernel,
        grid=(b, n_tiles),
        in_specs=[
            pl.BlockSpec((1, CONV_ROWS, width), lambda bb, i: (bb, i, 0)),
            pl.BlockSpec((1, SUBLANES, width),
                         lambda bb, i: (bb, jnp.maximum(i * sub_per_tile - 1, 0), 0)),
            pl.BlockSpec((1, SUBLANES, width),
                         lambda bb, i: (bb, jnp.minimum((i + 1) * sub_per_tile, last_sub), 0)),
            pl.BlockSpec((SUBLANES, width), lambda bb, i: (0, 0)),
        ],
        out_specs=[out_spec, out_spec, out_spec],
        out_shape=[out_sds, out_sds, out_sds],
        scratch_shapes=[pltpu.VMEM((CONV_ROWS + 2 * SUBLANES, width), F32)],
        compiler_params=pltpu.CompilerParams(
            dimension_semantics=("arbitrary", "arbitrary"), vmem_limit_bytes=VMEM_LIMIT),
        name="dn_conv",
    )(proj3d, proj3d, proj3d, conv_w)


def _split3(x):
    x1 = x.astype(BF16)
    r1 = x - x1.astype(F32)
    x2 = r1.astype(BF16)
    x3 = (r1 - x2.astype(F32)).astype(BF16)
    return x1, x2, x3


def _exact_left(m_bf16, x):
    return sum(jnp.dot(m_bf16, p, preferred_element_type=F32) for p in _split3(x))


def _exact_right(x, m_bf16):
    return sum(jnp.dot(p, m_bf16, preferred_element_type=F32) for p in _split3(x))


def _tri_inverse_minus_eye(x0, row, col):
    same = lambda s: (row // s) == (col // s)
    s = 2
    p = jnp.where(same(s), x0, 0.0)
    while s < TILE:
        m_off = jnp.where(same(2 * s) & jnp.logical_not(same(s)), -x0, 0.0)
        z = m_off + _dot(m_off, p)
        yield
        p = p - z - _dot(p, z)
        yield
        s *= 2
    return p


def _dn_chain(q, k, v, beta, gcc, gcr, tot, incl, diag, row, col, state_ref, o_ref, hs):
    decay = jnp.where(incl, jnp.exp(jnp.where(incl, gcc - gcr, 0.0)), 0.0)
    gram = _dot_nt(k, k)
    qk = _dot_nt(q, k)
    yield
    x0 = jnp.where(diag, 0.0, -(gram * decay * beta))
    p = yield from _tri_inverse_minus_eye(x0, row, col)
    egc = jnp.exp(gcc)
    rhs = jnp.concatenate([v * beta, k * (beta * egc)], axis=1)
    uw = rhs + _dot(p, rhs)
    yield
    auw = _dot(qk * decay, uw)
    kt = _dot_tn(k * jnp.exp(tot - gcc), uw)
    yield
    p_q = q * egc - auw[:, TILE:]
    state = state_ref[...]
    res = _dot(jnp.concatenate([kt[:, TILE:], p_q], axis=0), state)
    yield
    o_ref[0, :, hs] = res[TILE:] + auw[:, :TILE]
    state_ref[...] = jnp.exp(tot[0:1, :]) * state - res[:TILE] + kt[:, :TILE]


def _dn_kernel(qf_ref, kf_ref, vf_ref, gcf_ref, grf_ref,
               qb_ref, kb_ref, vb_ref, gcb_ref, grb_ref,
               alr_ref, dtr_ref, alc_ref, dtc_ref,
               of_ref, ob_ref, s_ref):
    @pl.when(pl.program_id(1) == 0)
    def _():
        s_ref[...] = jnp.zeros_like(s_ref)

    row = lax.broadcasted_iota(jnp.int32, (TILE, TILE), 0)
    col = lax.broadcasted_iota(jnp.int32, (TILE, TILE), 1)
    lower = (col <= row)
    upper = (col >= row)
    diag = (col == row)
    lower_bf = lower.astype(BF16)
    upper_bf = upper.astype(BF16)
    ones_bf = jnp.ones((TILE, TILE), BF16)

    streams = ((qf_ref, kf_ref, vf_ref, gcf_ref, grf_ref, of_ref, lower, lower_bf, upper_bf),
               (qb_ref, kb_ref, vb_ref, gcb_ref, grb_ref, ob_ref, upper, upper_bf, lower_bf))
    chains = []
    for d, (q_ref, k_ref, v_ref, gc_ref, gr_ref, o_ref, incl, cum_left, cum_right) in enumerate(streams):
        gate_c = gc_ref[0]
        gate_r = gr_ref[...]
        beta_c = jax.nn.sigmoid(gate_c)
        g_c = -jnp.exp(alr_ref[...]) * _softplus(gate_c + dtr_ref[...])
        g_r = -jnp.exp(alc_ref[...]) * _softplus(gate_r + dtc_ref[...])
        cum_c = _exact_left(cum_left, g_c)
        tot_c = _exact_left(ones_bf, g_c)
        cum_r = _exact_right(g_r, cum_right)
        for h in range(DN_HEADS):
            cb = d * DN_HEADS + h
            cg = N_GATE + cb
            hs = slice(h * LANES, (h + 1) * LANES)
            chains.append(_dn_chain(
                q_ref[0, :, hs], k_ref[0, :, hs], v_ref[0, :, hs],
                beta_c[:, cb:cb + 1], cum_c[:, cg:cg + 1], cum_r[cg:cg + 1, :], tot_c[:, cg:cg + 1],
                incl, diag, row, col, s_ref.at[cb], o_ref, hs))
    while chains:
        alive = []
        for chain in chains:
            try:
                next(chain)
                alive.append(chain)
            except StopIteration:
                pass
        chains = alive


def _dn_scan(q, k, v, proj3d, gate_t, al_row, dt_row, al_col, dt_col):
    b, s, _ = q.shape
    n_tiles = s // TILE
    gate_blk = COL_GATE // LANES
    fwd = lambda bb, i: (bb, i, 0)
    bwd = lambda bb, i: (bb, n_tiles - 1 - i, 0)
    qkv_f = pl.BlockSpec((1, TILE, DN_WIDTH), fwd)
    qkv_b = pl.BlockSpec((1, TILE, DN_WIDTH), bwd)
    gc_f = pl.BlockSpec((1, TILE, LANES), lambda bb, i: (bb, i, gate_blk))
    gc_b = pl.BlockSpec((1, TILE, LANES), lambda bb, i: (bb, n_tiles - 1 - i, gate_blk))
    gr_f = pl.BlockSpec((2 * N_GATE, TILE), lambda bb, i: (0, bb * n_tiles + i))
    gr_b = pl.BlockSpec((2 * N_GATE, TILE), lambda bb, i: (0, bb * n_tiles + n_tiles - 1 - i))
    row_p = pl.BlockSpec((1, LANES), lambda bb, i: (0, 0))
    col_p = pl.BlockSpec((2 * N_GATE, 1), lambda bb, i: (0, 0))
    out_sds = jax.ShapeDtypeStruct((b, s, DN_WIDTH), F32)
    return pl.pallas_call(
        _dn_kernel,
        grid=(b, n_tiles),
        in_specs=[qkv_f, qkv_f, qkv_f, gc_f, gr_f,
                  qkv_b, qkv_b, qkv_b, gc_b, gr_b,
                  row_p, row_p, col_p, col_p],
        out_specs=[qkv_f, qkv_b],
        out_shape=[out_sds, out_sds],
        scratch_shapes=[pltpu.VMEM((N_GATE, DN_HEAD_DIM, DN_HEAD_DIM), F32)],
        compiler_params=pltpu.CompilerParams(
            dimension_semantics=("arbitrary", "arbitrary"), vmem_limit_bytes=VMEM_LIMIT),
        name="dn_scan",
    )(q, k, v, proj3d, gate_t, q, k, v, proj3d, gate_t, al_row, dt_row, al_col, dt_col)


def _pair_rms(x, w, head_ones):
    x2 = x * x
    hi = x2.astype(BF16)
    lo = (x2 - hi.astype(F32)).astype(BF16)
    ss = (jnp.dot(hi, head_ones, preferred_element_type=F32)
          + jnp.dot(lo, head_ones, preferred_element_type=F32))
    return (x * lax.rsqrt(ss * (1.0 / SWA_HEAD_DIM) + EPS)) * w


def _pair_rope(x, cos, sin_signed, first_quarter):
    half = SWA_HEAD_DIM // 2
    rot = jnp.where(first_quarter, pltpu.roll(x, LANES - half, 1), pltpu.roll(x, half, 1))
    return x * cos + rot * sin_signed


def _swa_head(q_m, k_span, v_span, valid, sink, o_ref, lanes, needs_roll):
    sc = lax.dot_general(q_m, k_span, NT_DIMS, preferred_element_type=F32)
    yield
    sc = jnp.where(valid, sc, -1e30)
    mx = jnp.maximum(jnp.max(sc, axis=-1, keepdims=True), sink)
    yield
    p = jnp.exp(sc - mx)
    denom = jnp.sum(p, axis=-1, keepdims=True) + jnp.exp(sink - mx)
    yield
    pv = jnp.dot(p.astype(BF16), v_span, preferred_element_type=F32) / denom
    if needs_roll:
        pv = pltpu.roll(pv, SWA_HEAD_DIM, 1)
    o_ref[0, :, lanes] = pv[:, lanes.start % LANES:(lanes.start % LANES) + SWA_HEAD_DIM]


def _swa_kernel(q_ref, kc_ref, kn_ref, vp_ref, vc_ref, vn_ref,
                cc_ref, sc_ref, cn_ref, sn_ref,
                qw_ref, kw_ref, sink_ref, o_ref, ks_ref):
    i = pl.program_id(1)
    n = pl.num_programs(1)
    lane = lax.broadcasted_iota(jnp.int32, (TILE, LANES), 1)
    lo_half = lane < SWA_HEAD_DIM
    first_quarter = (lane % SWA_HEAD_DIM) < (SWA_HEAD_DIM // 2)

    sub = lax.broadcasted_iota(jnp.int32, (LANES, LANES), 0)
    head_ones = ((sub // SWA_HEAD_DIM) == (lane // SWA_HEAD_DIM)).astype(BF16)

    def prep(x, w, cos_ref, sin_ref):
        return _pair_rope(_pair_rms(x, w, head_ones), cos_ref[...], sin_ref[...], first_quarter)

    kw = kw_ref[...]

    @pl.when(i == 0)
    def _():
        ks_ref[0] = jnp.zeros((TILE, LANES), BF16)
        ks_ref[1] = prep(kc_ref[0], kw, cc_ref, sc_ref).astype(BF16)

    @pl.when(i > 0)
    def _():
        ks_ref[0] = ks_ref[1]
        ks_ref[1] = ks_ref[2]

    ks_ref[2] = prep(kn_ref[0], kw, cn_ref, sn_ref).astype(BF16)
    k_span = ks_ref[...].reshape(3 * TILE, LANES)
    v_span = jnp.concatenate([vp_ref[0], vc_ref[0], vn_ref[0]], axis=0).astype(BF16)
    span = 3 * TILE
    r = lax.broadcasted_iota(jnp.int32, (TILE, span), 0)
    c = lax.broadcasted_iota(jnp.int32, (TILE, span), 1)
    valid = (jnp.abs(c - WINDOW - r) <= WINDOW)
    valid = valid & ((c >= TILE) | (i > 0)) & ((c < 2 * TILE) | (i < n - 1))
    group = SWA_Q_HEADS // SWA_KV_HEADS
    qw = qw_ref[...]
    scale = SWA_HEAD_DIM ** -0.5
    heads = []
    for j in range(SWA_Q_HEADS // 2):
        q_pair = prep(q_ref[0, :, j * LANES:(j + 1) * LANES], qw, cc_ref, sc_ref) * scale
        q_swap = pltpu.roll(q_pair, SWA_HEAD_DIM, 1)
        for half in range(2):
            hq = 2 * j + half
            g = hq // group
            q_al = q_pair if half == g else q_swap
            q_m = jnp.where(lo_half if g == 0 else jnp.logical_not(lo_half), q_al, 0.0).astype(BF16)
            lanes = slice(hq * SWA_HEAD_DIM, (hq + 1) * SWA_HEAD_DIM)
            heads.append(_swa_head(q_m, k_span, v_span, valid, sink_ref[hq], o_ref, lanes, half != g))
    while heads:
        alive = []
        for head in heads:
            try:
                next(head)
                alive.append(head)
            except StopIteration:
                pass
        heads = alive


def _swa(proj3d, cos_t, sin_t, qw, kw, sinks):
    b, s, _ = proj3d.shape
    n_tiles = s // TILE
    kblk = COL_SWK // LANES
    vblk = COL_SWV // LANES
    prev = lambda i: jnp.maximum(i - 1, 0)
    nxt = lambda i: jnp.minimum(i + 1, n_tiles - 1)

    def kv_spec(blk, f):
        return pl.BlockSpec((1, TILE, LANES), lambda bb, i: (bb, f(i), blk))

    def tab_spec(f):
        return pl.BlockSpec((TILE, LANES), lambda bb, i: (f(i), 0))

    ident = lambda i: i
    row_p = pl.BlockSpec((1, LANES), lambda bb, i: (0, 0))
    return pl.pallas_call(
        _swa_kernel,
        grid=(b, n_tiles),
        in_specs=[
            pl.BlockSpec((1, TILE, SWA_WIDTH), lambda bb, i: (bb, i, COL_SWQ // SWA_WIDTH)),
            kv_spec(kblk, ident), kv_spec(kblk, nxt),
            kv_spec(vblk, prev), kv_spec(vblk, ident), kv_spec(vblk, nxt),
            tab_spec(ident), tab_spec(ident), tab_spec(nxt), tab_spec(nxt),
            row_p, row_p,
            pl.BlockSpec(memory_space=pltpu.SMEM),
        ],
        out_specs=pl.BlockSpec((1, TILE, SWA_WIDTH), lambda bb, i: (bb, i, 0)),
        out_shape=jax.ShapeDtypeStruct((b, s, SWA_WIDTH), F32),
        scratch_shapes=[pltpu.VMEM((3, TILE, LANES), BF16)],
        compiler_params=pltpu.CompilerParams(
            dimension_semantics=("arbitrary", "arbitrary"), vmem_limit_bytes=VMEM_LIMIT),
        name="swa",
    )(proj3d, proj3d, proj3d, proj3d, proj3d, proj3d,
      cos_t, sin_t, cos_t, sin_t, qw, kw, sinks)


def _out_kernel(of_ref, ob_ref, z_ref, sw_ref, swz_ref, x_ref, onw_ref, wo_ref, y_ref):
    dn = of_ref[...] + ob_ref[...]
    onw = onw_ref[...]
    parts = []
    for h in range(DN_HEADS):
        blk = dn[:, h * LANES:(h + 1) * LANES]
        ms = jnp.mean(blk * blk, axis=-1, keepdims=True)
        parts.append((blk * lax.rsqrt(ms + EPS)) * onw)
    dn_n = jnp.concatenate(parts, axis=1) * _silu(z_ref[...])
    sw = sw_ref[...] * _silu(swz_ref[...])
    mix = jnp.concatenate([dn_n, sw], axis=1).astype(BF16)
    y_ref[...] = x_ref[...] + jnp.dot(mix, wo_ref[...], preferred_element_type=F32)


def _out_projection(o_f, o_b, proj2d, sw, x2d, out_norm_w, w_out):
    rows = x2d.shape[0]
    half = lambda i: (i, 0)
    return pl.pallas_call(
        _out_kernel,
        grid=(rows // PROJ_ROWS,),
        in_specs=[
            pl.BlockSpec((PROJ_ROWS, DN_WIDTH), half),
            pl.BlockSpec((PROJ_ROWS, DN_WIDTH), half),
            pl.BlockSpec((PROJ_ROWS, DN_WIDTH), lambda i: (i, COL_DNZ // DN_WIDTH)),
            pl.BlockSpec((PROJ_ROWS, SWA_WIDTH), half),
            pl.BlockSpec((PROJ_ROWS, SWA_WIDTH), lambda i: (i, COL_SWZ // SWA_WIDTH)),
            pl.BlockSpec((PROJ_ROWS, D_MODEL), half),
            pl.BlockSpec((1, LANES), lambda i: (0, 0)),
            pl.BlockSpec((DN_WIDTH + SWA_WIDTH, D_MODEL), lambda i: (0, 0)),
        ],
        out_specs=pl.BlockSpec((PROJ_ROWS, D_MODEL), half),
        out_shape=jax.ShapeDtypeStruct((rows, D_MODEL), F32),
        compiler_params=pltpu.CompilerParams(
            dimension_semantics=("arbitrary",), vmem_limit_bytes=VMEM_LIMIT),
        name="out_proj",
    )(o_f, o_b, proj2d, sw, proj2d, x2d, out_norm_w, w_out)


def _rope_tables(seq):
    inv_freq = ROPE_THETA ** (-jnp.arange(0, SWA_HEAD_DIM, 2, dtype=F32) / SWA_HEAD_DIM)
    ang = jnp.arange(seq, dtype=F32)[:, None] * inv_freq[None, :]
    ang = jnp.concatenate([ang, ang], axis=-1)
    cos, sin = jnp.cos(ang), jnp.sin(ang)
    half = SWA_HEAD_DIM // 2
    sin_signed = jnp.concatenate([-sin[:, :half], sin[:, half:]], axis=-1)
    reps = LANES // SWA_HEAD_DIM
    return jnp.tile(cos, (1, reps)), jnp.tile(sin_signed, (1, reps))


def _layer(x, norm_w, w_in, conv_w, a_log, dt_bias, out_norm_w, q_norm_w, k_norm_w, sinks, w_out):
    b, s, _ = x.shape
    x2d = x.reshape(b * s, D_MODEL)
    o_q, o_k, o_v, o_z = 0, DN_WIDTH, 2 * DN_WIDTH, 3 * DN_WIDTH
    o_beta = 4 * DN_WIDTH
    o_decay = o_beta + N_GATE
    o_swq = o_decay + N_GATE
    o_swk = o_swq + SWA_WIDTH
    o_swv = o_swk + SWA_KV_WIDTH
    o_swz = o_swv + SWA_KV_WIDTH
    w_gate = w_in[:, o_beta:o_swq]
    w_perm = jnp.concatenate([
        w_in[:, o_q:o_beta],
        w_in[:, o_swq:o_swk], w_in[:, o_swz:o_swz + SWA_WIDTH],
        w_in[:, o_swk:o_swv], w_in[:, o_swv:o_swz],
        w_gate, jnp.zeros((D_MODEL, LANES - 2 * N_GATE), w_in.dtype)], axis=1).astype(BF16)
    proj2d, gate_t = _in_projection(x2d, norm_w.reshape(1, D_MODEL), w_perm, w_gate.T.astype(BF16))
    proj3d = proj2d.reshape(b, s, PROJ_WIDTH)

    conv_pad = jnp.concatenate([conv_w, jnp.zeros((SUBLANES - DN_CONV, 3 * DN_WIDTH), conv_w.dtype)], axis=0)
    q, k, v = _dn_conv(proj3d, conv_pad)

    a_flat = a_log.reshape(N_GATE).astype(F32)
    d_flat = dt_bias.reshape(N_GATE).astype(F32)
    zeros8 = jnp.zeros((N_GATE,), F32)
    pad_row = jnp.zeros((LANES - 2 * N_GATE,), F32)
    al_row = jnp.concatenate([zeros8, a_flat, pad_row]).reshape(1, LANES)
    dt_row = jnp.concatenate([zeros8, d_flat, pad_row]).reshape(1, LANES)
    al_col = jnp.concatenate([zeros8, a_flat]).reshape(2 * N_GATE, 1)
    dt_col = jnp.concatenate([zeros8, d_flat]).reshape(2 * N_GATE, 1)
    o_f, o_b = _dn_scan(q, k, v, proj3d, gate_t, al_row, dt_row, al_col, dt_col)

    cos_t, sin_t = _rope_tables(s)
    reps = LANES // SWA_HEAD_DIM
    sw = _swa(proj3d, cos_t, sin_t,
              jnp.tile(q_norm_w.astype(F32), reps).reshape(1, LANES),
              jnp.tile(k_norm_w.astype(F32), reps).reshape(1, LANES),
              sinks.astype(F32))

    y = _out_projection(o_f.reshape(b * s, DN_WIDTH), o_b.reshape(b * s, DN_WIDTH), proj2d,
                        sw.reshape(b * s, SWA_WIDTH), x2d,
                        out_norm_w.reshape(1, LANES).astype(F32), w_out.astype(BF16))
    return y.reshape(b, s, D_MODEL)


def kernel(x, norm_w, w_in, dn_conv_w, dn_a_log, dn_dt_bias, dn_out_norm_w,
           swa_q_norm_w, swa_k_norm_w, swa_sinks, w_out):
    for l in range(norm_w.shape[0]):
        x = _layer(x, norm_w[l], w_in[l], dn_conv_w[l], dn_a_log[l], dn_dt_bias[l], dn_out_norm_w[l],
                   swa_q_norm_w[l], swa_k_norm_w[l], swa_sinks[l], w_out[l])
    return x
```

```python
import jax
import jax.numpy as jnp
from jax import lax
from jax.experimental import pallas as pl
from jax.experimental.pallas import tpu as pltpu

D_MODEL = 1024
DN_HEADS = 4
DN_HEAD_DIM = 128
DN_WIDTH = DN_HEADS * DN_HEAD_DIM
DN_CONV = 5
N_DIR = 2
N_GATE = N_DIR * DN_HEADS
SWA_Q_HEADS = 8
SWA_KV_HEADS = 2
SWA_HEAD_DIM = 64
SWA_WIDTH = SWA_Q_HEADS * SWA_HEAD_DIM
SWA_KV_WIDTH = SWA_KV_HEADS * SWA_HEAD_DIM
WINDOW = 128
ROPE_THETA = 10000.0
EPS = 1e-6

LANES = 128
SUBLANES = 8
TILE = 128
PROJ_ROWS = 512
CONV_ROWS = 512
VMEM_LIMIT = 48 * 1024 * 1024

COL_QKV = 0
COL_DNZ = 3 * DN_WIDTH
COL_SWQ = COL_DNZ + DN_WIDTH
COL_SWZ = COL_SWQ + SWA_WIDTH
COL_SWK = COL_SWZ + SWA_WIDTH
COL_SWV = COL_SWK + SWA_KV_WIDTH
PROJ_WIDTH = COL_SWV + SWA_KV_WIDTH
HALO = 16

F32 = jnp.float32
BF16 = jnp.bfloat16
NT_DIMS = (((1,), (1,)), ((), ()))
TN_DIMS = (((0,), (0,)), ((), ()))


def _silu(x):
    return x * jax.nn.sigmoid(x)


def _softplus(x):
    return jnp.maximum(x, 0.0) + jnp.log1p(jnp.exp(-jnp.abs(x)))


def _dot(a, b):
    return jnp.dot(a.astype(BF16), b.astype(BF16), preferred_element_type=F32)


def _dot_nt(a, b):
    return lax.dot_general(a.astype(BF16), b.astype(BF16), NT_DIMS, preferred_element_type=F32)


def _dot_tn(a, b):
    return lax.dot_general(a.astype(BF16), b.astype(BF16), TN_DIMS, preferred_element_type=F32)


def _proj_kernel(x_ref, nw_ref, w_ref, wg_ref, wgt_ref, proj_ref, gc_ref, gt_ref):
    x = x_ref[...]
    ms = jnp.mean(x * x, axis=-1, keepdims=True)
    h = ((x * lax.rsqrt(ms + EPS)) * nw_ref[...]).astype(BF16)
    proj_ref[...] = jnp.dot(h, w_ref[...], preferred_element_type=F32).astype(BF16)
    gc_ref[...] = jnp.dot(h, wg_ref[...], preferred_element_type=F32)
    gt_ref[...] = lax.dot_general(wgt_ref[...], h, NT_DIMS, preferred_element_type=F32)


def _in_projection(x2d, norm_w, w_perm, w_gate, w_gate_t):
    rows = x2d.shape[0]
    return pl.pallas_call(
        _proj_kernel,
        grid=(rows // PROJ_ROWS,),
        in_specs=[
            pl.BlockSpec((PROJ_ROWS, D_MODEL), lambda i: (i, 0)),
            pl.BlockSpec((1, D_MODEL), lambda i: (0, 0)),
            pl.BlockSpec((D_MODEL, PROJ_WIDTH), lambda i: (0, 0)),
            pl.BlockSpec((D_MODEL, LANES), lambda i: (0, 0)),
            pl.BlockSpec((2 * N_GATE, D_MODEL), lambda i: (0, 0)),
        ],
        out_specs=[
            pl.BlockSpec((PROJ_ROWS, PROJ_WIDTH), lambda i: (i, 0)),
            pl.BlockSpec((PROJ_ROWS, LANES), lambda i: (i, 0)),
            pl.BlockSpec((2 * N_GATE, PROJ_ROWS), lambda i: (0, i)),
        ],
        out_shape=[
            jax.ShapeDtypeStruct((rows, PROJ_WIDTH), BF16),
            jax.ShapeDtypeStruct((rows, LANES), F32),
            jax.ShapeDtypeStruct((2 * N_GATE, rows), F32),
        ],
        compiler_params=pltpu.CompilerParams(
            dimension_semantics=("arbitrary",), vmem_limit_bytes=VMEM_LIMIT),
        name="in_proj",
    )(x2d, norm_w, w_perm, w_gate, w_gate_t)


def _conv_kernel(cur_ref, prev_ref, next_ref, cw_ref, q_ref, k_ref, v_ref, buf_ref):
    i = pl.program_id(1)
    n = pl.num_programs(1)
    rows = cur_ref.shape[1]
    pad = (DN_CONV - 1) // 2
    buf_ref[HALO:HALO + rows, :] = cur_ref[0].astype(F32)
    buf_ref[0:HALO, :] = jnp.where(i > 0, prev_ref[0].astype(F32), 0.0)
    buf_ref[HALO + rows:2 * HALO + rows, :] = jnp.where(i < n - 1, next_ref[0].astype(F32), 0.0)
    outs = (q_ref, k_ref, v_ref)
    for c in range(3 * DN_HEADS):
        lo = c * LANES
        acc = None
        for j in range(DN_CONV):
            r0 = HALO - pad + j
            term = buf_ref[r0:r0 + rows, lo:lo + LANES] * cw_ref[j:j + 1, lo:lo + LANES]
            acc = term if acc is None else acc + term
        y = _silu(acc)
        part, head = divmod(c, DN_HEADS)
        if part < 2:
            y = y * lax.rsqrt(jnp.sum(y * y, axis=-1, keepdims=True) + EPS)
        if part == 0:
            y = y * (DN_HEAD_DIM ** -0.5)
        outs[part][0, :, head * LANES:(head + 1) * LANES] = y.astype(BF16)


def _dn_conv(proj3d, conv_w):
    b, s, _ = proj3d.shape
    n_tiles = s // CONV_ROWS
    sub_per_tile = CONV_ROWS // HALO
    last_sub = s // HALO - 1
    width = 3 * DN_WIDTH
    out_spec = pl.BlockSpec((1, CONV_ROWS, DN_WIDTH), lambda bb, i: (bb, i, 0))
    out_sds = jax.ShapeDtypeStruct((b, s, DN_WIDTH), BF16)
    return pl.pallas_call(
        _conv_kernel,
        grid=(b, n_tiles),
        in_specs=[
            pl.BlockSpec((1, CONV_ROWS, width), lambda bb, i: (bb, i, 0)),
            pl.BlockSpec((1, HALO, width),
                         lambda bb, i: (bb, jnp.maximum(i * sub_per_tile - 1, 0), 0)),
            pl.BlockSpec((1, HALO, width),
                         lambda bb, i: (bb, jnp.minimum((i + 1) * sub_per_tile, last_sub), 0)),
            pl.BlockSpec((SUBLANES, width), lambda bb, i: (0, 0)),
        ],
        out_specs=[out_spec, out_spec, out_spec],
        out_shape=[out_sds, out_sds, out_sds],
        scratch_shapes=[pltpu.VMEM((CONV_ROWS + 2 * HALO, width), F32)],
        compiler_params=pltpu.CompilerParams(
            dimension_semantics=("arbitrary", "arbitrary"), vmem_limit_bytes=VMEM_LIMIT),
        name="dn_conv",
    )(proj3d, proj3d, proj3d, conv_w)


def _split3(x):
    x1 = x.astype(BF16)
    r1 = x - x1.astype(F32)
    x2 = r1.astype(BF16)
    x3 = (r1 - x2.astype(F32)).astype(BF16)
    return x1, x2, x3


def _exact_left(m_bf16, x):
    return sum(jnp.dot(m_bf16, p, preferred_element_type=F32) for p in _split3(x))


def _exact_right(x, m_bf16):
    return sum(jnp.dot(p, m_bf16, preferred_element_type=F32) for p in _split3(x))


def _tri_inverse_minus_eye(x0, row, col):
    same = lambda s: (row // s) == (col // s)
    s = 2
    p = jnp.where(same(s), x0, 0.0)
    while s < TILE:
        m_off = jnp.where(same(2 * s) & jnp.logical_not(same(s)), -x0, 0.0)
        z = m_off + _dot(m_off, p)
        yield
        p = p - z - _dot(p, z)
        yield
        s *= 2
    return p


def _dn_chain(q, k, v, beta, gcc, gcr, tot, incl, diag, row, col, state_ref, o_ref, hs):
    decay = jnp.where(incl, jnp.exp(jnp.where(incl, gcc - gcr, 0.0)), 0.0)
    gram = _dot_nt(k, k)
    qk = _dot_nt(q, k)
    yield
    x0 = jnp.where(diag, 0.0, -(gram * decay * beta))
    p = yield from _tri_inverse_minus_eye(x0, row, col)
    egc = jnp.exp(gcc)
    rhs = jnp.concatenate([v * beta, k * (beta * egc)], axis=1)
    uw = rhs + _dot(p, rhs)
    yield
    auw = _dot(qk * decay, uw)
    kt = _dot_tn(k * jnp.exp(tot - gcc), uw)
    yield
    p_q = q * egc - auw[:, TILE:]
    state = state_ref[...]
    res = _dot(jnp.concatenate([kt[:, TILE:], p_q], axis=0), state)
    yield
    o_ref[0, :, hs] = (res[TILE:] + auw[:, :TILE]).astype(BF16)
    state_ref[...] = jnp.exp(tot[0:1, :]) * state - res[:TILE] + kt[:, :TILE]


def _dn_kernel(qf_ref, kf_ref, vf_ref, gcf_ref, grf_ref,
               qb_ref, kb_ref, vb_ref, gcb_ref, grb_ref,
               alr_ref, dtr_ref, alc_ref, dtc_ref,
               of_ref, ob_ref, s_ref):
    @pl.when(pl.program_id(1) == 0)
    def _():
        s_ref[...] = jnp.zeros_like(s_ref)

    row = lax.broadcasted_iota(jnp.int32, (TILE, TILE), 0)
    col = lax.broadcasted_iota(jnp.int32, (TILE, TILE), 1)
    lower = (col <= row)
    upper = (col >= row)
    diag = (col == row)
    lower_bf = lower.astype(BF16)
    upper_bf = upper.astype(BF16)
    ones_bf = jnp.ones((TILE, TILE), BF16)

    streams = ((qf_ref, kf_ref, vf_ref, gcf_ref, grf_ref, of_ref, lower, lower_bf, upper_bf),
               (qb_ref, kb_ref, vb_ref, gcb_ref, grb_ref, ob_ref, upper, upper_bf, lower_bf))
    chains = []
    for d, (q_ref, k_ref, v_ref, gc_ref, gr_ref, o_ref, incl, cum_left, cum_right) in enumerate(streams):
        gate_c = gc_ref[0]
        gate_r = gr_ref[...]
        beta_c = jax.nn.sigmoid(gate_c)
        g_c = -jnp.exp(alr_ref[...]) * _softplus(gate_c + dtr_ref[...])
        g_r = -jnp.exp(alc_ref[...]) * _softplus(gate_r + dtc_ref[...])
        cum_c = _exact_left(cum_left, g_c)
        tot_c = _exact_left(ones_bf, g_c)
        cum_r = _exact_right(g_r, cum_right)
        for h in range(DN_HEADS):
            cb = d * DN_HEADS + h
            cg = N_GATE + cb
            hs = slice(h * LANES, (h + 1) * LANES)
            chains.append(_dn_chain(
                q_ref[0, :, hs].astype(F32), k_ref[0, :, hs].astype(F32), v_ref[0, :, hs].astype(F32),
                beta_c[:, cb:cb + 1], cum_c[:, cg:cg + 1], cum_r[cg:cg + 1, :], tot_c[:, cg:cg + 1],
                incl, diag, row, col, s_ref.at[cb], o_ref, hs))
    while chains:
        alive = []
        for chain in chains:
            try:
                next(chain)
                alive.append(chain)
            except StopIteration:
                pass
        chains = alive


def _dn_scan(q, k, v, gate_c, gate_t, al_row, dt_row, al_col, dt_col):
    b, s, _ = q.shape
    n_tiles = s // TILE
    fwd = lambda bb, i: (bb, i, 0)
    bwd = lambda bb, i: (bb, n_tiles - 1 - i, 0)
    qkv_f = pl.BlockSpec((1, TILE, DN_WIDTH), fwd)
    qkv_b = pl.BlockSpec((1, TILE, DN_WIDTH), bwd)
    gc_f = pl.BlockSpec((1, TILE, LANES), fwd)
    gc_b = pl.BlockSpec((1, TILE, LANES), bwd)
    gr_f = pl.BlockSpec((2 * N_GATE, TILE), lambda bb, i: (0, bb * n_tiles + i))
    gr_b = pl.BlockSpec((2 * N_GATE, TILE), lambda bb, i: (0, bb * n_tiles + n_tiles - 1 - i))
    row_p = pl.BlockSpec((1, LANES), lambda bb, i: (0, 0))
    col_p = pl.BlockSpec((2 * N_GATE, 1), lambda bb, i: (0, 0))
    out_sds = jax.ShapeDtypeStruct((b, s, DN_WIDTH), BF16)
    return pl.pallas_call(
        _dn_kernel,
        grid=(b, n_tiles),
        in_specs=[qkv_f, qkv_f, qkv_f, gc_f, gr_f,
                  qkv_b, qkv_b, qkv_b, gc_b, gr_b,
                  row_p, row_p, col_p, col_p],
        out_specs=[qkv_f, qkv_b],
        out_shape=[out_sds, out_sds],
        scratch_shapes=[pltpu.VMEM((N_GATE, DN_HEAD_DIM, DN_HEAD_DIM), F32)],
        compiler_params=pltpu.CompilerParams(
            dimension_semantics=("arbitrary", "arbitrary"), vmem_limit_bytes=VMEM_LIMIT),
        name="dn_scan",
    )(q, k, v, gate_c, gate_t, q, k, v, gate_c, gate_t, al_row, dt_row, al_col, dt_col)


def _pair_rms(x, w, head_ones):
    x2 = x * x
    hi = x2.astype(BF16)
    lo = (x2 - hi.astype(F32)).astype(BF16)
    ss = (jnp.dot(hi, head_ones, preferred_element_type=F32)
          + jnp.dot(lo, head_ones, preferred_element_type=F32))
    return (x * lax.rsqrt(ss * (1.0 / SWA_HEAD_DIM) + EPS)) * w


def _pair_rope(x, cos, sin_signed, first_quarter):
    half = SWA_HEAD_DIM // 2
    rot = jnp.where(first_quarter, pltpu.roll(x, LANES - half, 1), pltpu.roll(x, half, 1))
    return x * cos + rot * sin_signed


def _swa_head(q_m, k_span, v_span, valid, sink, o_ref, lanes, needs_roll):
    sc = lax.dot_general(q_m, k_span, NT_DIMS, preferred_element_type=F32)
    yield
    sc = jnp.where(valid, sc, -1e30)
    mx = jnp.maximum(jnp.max(sc, axis=-1, keepdims=True), sink)
    yield
    p = jnp.exp(sc - mx)
    denom = jnp.sum(p, axis=-1, keepdims=True) + jnp.exp(sink - mx)
    yield
    pv = jnp.dot(p.astype(BF16), v_span, preferred_element_type=F32) / denom
    if needs_roll:
        pv = pltpu.roll(pv, SWA_HEAD_DIM, 1)
    o_ref[0, :, lanes] = pv[:, lanes.start % LANES:(lanes.start % LANES) + SWA_HEAD_DIM].astype(BF16)


def _swa_kernel(q_ref, kc_ref, kn_ref, vp_ref, vc_ref, vn_ref,
                cc_ref, sc_ref, cn_ref, sn_ref,
                qw_ref, kw_ref, sink_ref, o_ref, ks_ref):
    i = pl.program_id(1)
    n = pl.num_programs(1)
    lane = lax.broadcasted_iota(jnp.int32, (TILE, LANES), 1)
    lo_half = lane < SWA_HEAD_DIM
    first_quarter = (lane % SWA_HEAD_DIM) < (SWA_HEAD_DIM // 2)

    sub = lax.broadcasted_iota(jnp.int32, (LANES, LANES), 0)
    head_ones = ((sub // SWA_HEAD_DIM) == (lane // SWA_HEAD_DIM)).astype(BF16)

    def prep(x, w, cos_ref, sin_ref):
        return _pair_rope(_pair_rms(x, w, head_ones), cos_ref[...], sin_ref[...], first_quarter)

    kw = kw_ref[...]

    @pl.when(i == 0)
    def _():
        ks_ref[0] = jnp.zeros((TILE, LANES), BF16)
        ks_ref[1] = prep(kc_ref[0].astype(F32), kw, cc_ref, sc_ref).astype(BF16)

    @pl.when(i > 0)
    def _():
        ks_ref[0] = ks_ref[1]
        ks_ref[1] = ks_ref[2]

    ks_ref[2] = prep(kn_ref[0].astype(F32), kw, cn_ref, sn_ref).astype(BF16)
    k_span = ks_ref[...].reshape(3 * TILE, LANES)
    v_span = jnp.concatenate([vp_ref[0], vc_ref[0], vn_ref[0]], axis=0)
    span = 3 * TILE
    r = lax.broadcasted_iota(jnp.int32, (TILE, span), 0)
    c = lax.broadcasted_iota(jnp.int32, (TILE, span), 1)
    valid = (jnp.abs(c - WINDOW - r) <= WINDOW)
    valid = valid & ((c >= TILE) | (i > 0)) & ((c < 2 * TILE) | (i < n - 1))
    group = SWA_Q_HEADS // SWA_KV_HEADS
    qw = qw_ref[...]
    scale = SWA_HEAD_DIM ** -0.5
    heads = []
    for j in range(SWA_Q_HEADS // 2):
        q_pair = prep(q_ref[0, :, j * LANES:(j + 1) * LANES].astype(F32), qw, cc_ref, sc_ref) * scale
        q_swap = pltpu.roll(q_pair, SWA_HEAD_DIM, 1)
        for half in range(2):
            hq = 2 * j + half
            g = hq // group
            q_al = q_pair if half == g else q_swap
            q_m = jnp.where(lo_half if g == 0 else jnp.logical_not(lo_half), q_al, 0.0).astype(BF16)
            lanes = slice(hq * SWA_HEAD_DIM, (hq + 1) * SWA_HEAD_DIM)
            heads.append(_swa_head(q_m, k_span, v_span, valid, sink_ref[hq], o_ref, lanes, half != g))
    while heads:
        alive = []
        for head in heads:
            try:
                next(head)
                alive.append(head)
            except StopIteration:
                pass
        heads = alive


def _swa(proj3d, cos_t, sin_t, qw, kw, sinks):
    b, s, _ = proj3d.shape
    n_tiles = s // TILE
    kblk = COL_SWK // LANES
    vblk = COL_SWV // LANES
    prev = lambda i: jnp.maximum(i - 1, 0)
    nxt = lambda i: jnp.minimum(i + 1, n_tiles - 1)

    def kv_spec(blk, f):
        return pl.BlockSpec((1, TILE, LANES), lambda bb, i: (bb, f(i), blk))

    def tab_spec(f):
        return pl.BlockSpec((TILE, LANES), lambda bb, i: (f(i), 0))

    ident = lambda i: i
    row_p = pl.BlockSpec((1, LANES), lambda bb, i: (0, 0))
    return pl.pallas_call(
        _swa_kernel,
        grid=(b, n_tiles),
        in_specs=[
            pl.BlockSpec((1, TILE, SWA_WIDTH), lambda bb, i: (bb, i, COL_SWQ // SWA_WIDTH)),
            kv_spec(kblk, ident), kv_spec(kblk, nxt),
            kv_spec(vblk, prev), kv_spec(vblk, ident), kv_spec(vblk, nxt),
            tab_spec(ident), tab_spec(ident), tab_spec(nxt), tab_spec(nxt),
            row_p, row_p,
            pl.BlockSpec(memory_space=pltpu.SMEM),
        ],
        out_specs=pl.BlockSpec((1, TILE, SWA_WIDTH), lambda bb, i: (bb, i, 0)),
        out_shape=jax.ShapeDtypeStruct((b, s, SWA_WIDTH), BF16),
        scratch_shapes=[pltpu.VMEM((3, TILE, LANES), BF16)],
        compiler_params=pltpu.CompilerParams(
            dimension_semantics=("arbitrary", "arbitrary"), vmem_limit_bytes=VMEM_LIMIT),
        name="swa",
    )(proj3d, proj3d, proj3d, proj3d, proj3d, proj3d,
      cos_t, sin_t, cos_t, sin_t, qw, kw, sinks)


def _out_kernel(of_ref, ob_ref, z_ref, sw_ref, swz_ref, x_ref, onw_ref, wo_ref, y_ref):
    dn = of_ref[...].astype(F32) + ob_ref[...].astype(F32)
    onw = onw_ref[...]
    parts = []
    for h in range(DN_HEADS):
        blk = dn[:, h * LANES:(h + 1) * LANES]
        ms = jnp.mean(blk * blk, axis=-1, keepdims=True)
        parts.append((blk * lax.rsqrt(ms + EPS)) * onw)
    dn_n = jnp.concatenate(parts, axis=1) * _silu(z_ref[...].astype(F32))
    sw = sw_ref[...].astype(F32) * _silu(swz_ref[...].astype(F32))
    mix = jnp.concatenate([dn_n, sw], axis=1).astype(BF16)
    y_ref[...] = x_ref[...] + jnp.dot(mix, wo_ref[...], preferred_element_type=F32)


def _out_projection(o_f, o_b, proj2d, sw, x2d, out_norm_w, w_out):
    rows = x2d.shape[0]
    half = lambda i: (i, 0)
    return pl.pallas_call(
        _out_kernel,
        grid=(rows // PROJ_ROWS,),
        in_specs=[
            pl.BlockSpec((PROJ_ROWS, DN_WIDTH), half),
            pl.BlockSpec((PROJ_ROWS, DN_WIDTH), half),
            pl.BlockSpec((PROJ_ROWS, DN_WIDTH), lambda i: (i, COL_DNZ // DN_WIDTH)),
            pl.BlockSpec((PROJ_ROWS, SWA_WIDTH), half),
            pl.BlockSpec((PROJ_ROWS, SWA_WIDTH), lambda i: (i, COL_SWZ // SWA_WIDTH)),
            pl.BlockSpec((PROJ_ROWS, D_MODEL), half),
            pl.BlockSpec((1, LANES), lambda i: (0, 0)),
            pl.BlockSpec((DN_WIDTH + SWA_WIDTH, D_MODEL), lambda i: (0, 0)),
        ],
        out_specs=pl.BlockSpec((PROJ_ROWS, D_MODEL), half),
        out_shape=jax.ShapeDtypeStruct((rows, D_MODEL), F32),
        compiler_params=pltpu.CompilerParams(
            dimension_semantics=("arbitrary",), vmem_limit_bytes=VMEM_LIMIT),
        name="out_proj",
    )(o_f, o_b, proj2d, sw, proj2d, x2d, out_norm_w, w_out)


def _rope_tables(seq):
    inv_freq = ROPE_THETA ** (-jnp.arange(0, SWA_HEAD_DIM, 2, dtype=F32) / SWA_HEAD_DIM)
    ang = jnp.arange(seq, dtype=F32)[:, None] * inv_freq[None, :]
    ang = jnp.concatenate([ang, ang], axis=-1)
    cos, sin = jnp.cos(ang), jnp.sin(ang)
    half = SWA_HEAD_DIM // 2
    sin_signed = jnp.concatenate([-sin[:, :half], sin[:, half:]], axis=-1)
    reps = LANES // SWA_HEAD_DIM
    return jnp.tile(cos, (1, reps)), jnp.tile(sin_signed, (1, reps))


def _layer(x, norm_w, w_in, conv_w, a_log, dt_bias, out_norm_w, q_norm_w, k_norm_w, sinks, w_out):
    b, s, _ = x.shape
    x2d = x.reshape(b * s, D_MODEL)
    o_q, o_k, o_v, o_z = 0, DN_WIDTH, 2 * DN_WIDTH, 3 * DN_WIDTH
    o_beta = 4 * DN_WIDTH
    o_decay = o_beta + N_GATE
    o_swq = o_decay + N_GATE
    o_swk = o_swq + SWA_WIDTH
    o_swv = o_swk + SWA_KV_WIDTH
    o_swz = o_swv + SWA_KV_WIDTH
    w_gate = w_in[:, o_beta:o_swq]
    w_perm = jnp.concatenate([
        w_in[:, o_q:o_beta],
        w_in[:, o_swq:o_swk], w_in[:, o_swz:o_swz + SWA_WIDTH],
        w_in[:, o_swk:o_swv], w_in[:, o_swv:o_swz]], axis=1).astype(BF16)
    w_gate_pad = jnp.concatenate(
        [w_gate, jnp.zeros((D_MODEL, LANES - 2 * N_GATE), w_in.dtype)], axis=1).astype(BF16)
    proj2d, gate_c, gate_t = _in_projection(x2d, norm_w.reshape(1, D_MODEL), w_perm, w_gate_pad,
                                            w_gate.T.astype(BF16))
    proj3d = proj2d.reshape(b, s, PROJ_WIDTH)

    conv_pad = jnp.concatenate([conv_w, jnp.zeros((SUBLANES - DN_CONV, 3 * DN_WIDTH), conv_w.dtype)], axis=0)
    q, k, v = _dn_conv(proj3d, conv_pad)

    a_flat = a_log.reshape(N_GATE).astype(F32)
    d_flat = dt_bias.reshape(N_GATE).astype(F32)
    zeros8 = jnp.zeros((N_GATE,), F32)
    pad_row = jnp.zeros((LANES - 2 * N_GATE,), F32)
    al_row = jnp.concatenate([zeros8, a_flat, pad_row]).reshape(1, LANES)
    dt_row = jnp.concatenate([zeros8, d_flat, pad_row]).reshape(1, LANES)
    al_col = jnp.concatenate([zeros8, a_flat]).reshape(2 * N_GATE, 1)
    dt_col = jnp.concatenate([zeros8, d_flat]).reshape(2 * N_GATE, 1)
    o_f, o_b = _dn_scan(q, k, v, gate_c.reshape(b, s, LANES), gate_t, al_row, dt_row, al_col, dt_col)

    cos_t, sin_t = _rope_tables(s)
    reps = LANES // SWA_HEAD_DIM
    sw = _swa(proj3d, cos_t, sin_t,
              jnp.tile(q_norm_w.astype(F32), reps).reshape(1, LANES),
              jnp.tile(k_norm_w.astype(F32), reps).reshape(1, LANES),
              sinks.astype(F32))

    y = _out_projection(o_f.reshape(b * s, DN_WIDTH), o_b.reshape(b * s, DN_WIDTH), proj2d,
                        sw.reshape(b * s, SWA_WIDTH), x2d,
                        out_norm_w.reshape(1, LANES).astype(F32), w_out.astype(BF16))
    return y.reshape(b, s, D_MODEL)


def kernel(x, norm_w, w_in, dn_conv_w, dn_a_log, dn_dt_bias, dn_out_norm_w,
           swa_q_norm_w, swa_k_norm_w, swa_sinks, w_out):
    for l in range(norm_w.shape[0]):
        x = _layer(x, norm_w[l], w_in[l], dn_conv_w[l], dn_a_log[l], dn_dt_bias[l], dn_out_norm_w[l],
                   swa_q_norm_w[l], swa_k_norm_w[l], swa_sinks[l], w_out[l])
    return x
```

```python
import functools

import jax
import jax.numpy as jnp
from jax import lax
from jax.experimental import pallas as pl
from jax.experimental.pallas import tpu as pltpu

D_MODEL = 1024
DN_HEADS = 4
DN_HEAD_DIM = 128
DN_WIDTH = DN_HEADS * DN_HEAD_DIM
DN_CONV = 5
N_DIR = 2
N_GATE = N_DIR * DN_HEADS
SWA_Q_HEADS = 8
SWA_KV_HEADS = 2
SWA_HEAD_DIM = 64
SWA_WIDTH = SWA_Q_HEADS * SWA_HEAD_DIM
SWA_KV_WIDTH = SWA_KV_HEADS * SWA_HEAD_DIM
WINDOW = 128
ROPE_THETA = 10000.0
EPS = 1e-6

LANES = 128
SUBLANES = 8
TILE = 128
PROJ_ROWS = 512
VMEM_LIMIT = 48 * 1024 * 1024

COL_DNZ = 0
COL_SWQ = COL_DNZ + DN_WIDTH
COL_SWZ = COL_SWQ + SWA_WIDTH
COL_SWK = COL_SWZ + SWA_WIDTH
COL_SWV = COL_SWK + SWA_KV_WIDTH
PROJ_WIDTH = 3 * DN_WIDTH + COL_SWV + SWA_KV_WIDTH
HALO = 16

F32 = jnp.float32
BF16 = jnp.bfloat16
NT_DIMS = (((1,), (1,)), ((), ()))
TN_DIMS = (((0,), (0,)), ((), ()))


def _silu(x):
    return x * jax.nn.sigmoid(x)


def _softplus(x):
    return jnp.maximum(x, 0.0) + jnp.log1p(jnp.exp(-jnp.abs(x)))


def _dot(a, b):
    return jnp.dot(a.astype(BF16), b.astype(BF16), preferred_element_type=F32)


def _dot_nt(a, b):
    return lax.dot_general(a.astype(BF16), b.astype(BF16), NT_DIMS, preferred_element_type=F32)


def _dot_tn(a, b):
    return lax.dot_general(a.astype(BF16), b.astype(BF16), TN_DIMS, preferred_element_type=F32)


def _proj_conv_kernel(tiles_per_seq, x_ref, nw_ref, w_ref, wg_ref, wgt_ref, cw_ref,
                      rest_ref, gc_ref, gt_ref, q_ref, k_ref, v_ref, buf_ref, new_ref, h_ref):
    i = pl.program_id(0)
    n_tiles = pl.num_programs(0) - 2
    rows = x_ref.shape[0]
    pad = (DN_CONV - 1) // 2
    qkv_w = 3 * DN_WIDTH

    @pl.when(i == 0)
    def _():
        buf_ref[...] = jnp.zeros_like(buf_ref)
        h_ref[...] = jnp.zeros_like(h_ref)

    h = h_ref[...]
    t = i - 1
    boundary = (t % tiles_per_seq == 0) | (t < 0) | (t >= n_tiles)
    head = jnp.dot(h[0:HALO, :], w_ref[:, 0:qkv_w], preferred_element_type=F32)
    buf_ref[HALO + rows:2 * HALO + rows, :] = jnp.where(boundary, 0.0, head)
    new_ref[...] = jnp.dot(h, w_ref[:, 0:qkv_w], preferred_element_type=F32)

    gc_ref[...] = jnp.dot(h, wg_ref[...], preferred_element_type=F32)
    gt_ref[...] = lax.dot_general(wgt_ref[...], h, NT_DIMS, preferred_element_type=F32)

    outs = (q_ref, k_ref, v_ref)
    rest_w = rest_ref.shape[1]
    chunk = 2 * LANES
    n_chunks = rest_w // chunk
    n_blocks = 3 * DN_HEADS
    for c in range(n_blocks):
        for r in range(c * n_chunks // n_blocks, (c + 1) * n_chunks // n_blocks):
            cols = slice(r * chunk, (r + 1) * chunk)
            rest_ref[:, cols] = jnp.dot(h, w_ref[:, qkv_w + r * chunk:qkv_w + (r + 1) * chunk],
                                        preferred_element_type=F32).astype(BF16)
        lo = c * LANES
        acc = None
        for j in range(DN_CONV):
            r0 = HALO - pad + j
            term = buf_ref[r0:r0 + rows, lo:lo + LANES] * cw_ref[j:j + 1, lo:lo + LANES]
            acc = term if acc is None else acc + term
        y = _silu(acc)
        part, head = divmod(c, DN_HEADS)
        if part < 2:
            y = y * lax.rsqrt(jnp.sum(y * y, axis=-1, keepdims=True) + EPS)
        if part == 0:
            y = y * (DN_HEAD_DIM ** -0.5)
        outs[part][:, head * LANES:(head + 1) * LANES] = y.astype(BF16)

    buf_ref[0:HALO, :] = jnp.where(boundary, 0.0, buf_ref[rows:rows + HALO, :])
    buf_ref[HALO:HALO + rows, :] = new_ref[...]

    x = x_ref[...]
    ms = jnp.mean(x * x, axis=-1, keepdims=True)
    h_ref[...] = ((x * lax.rsqrt(ms + EPS)) * nw_ref[...]).astype(BF16)


def _in_projection_conv(x2d, seq, norm_w, w_perm, w_gate, w_gate_t, conv_w):
    rows = x2d.shape[0]
    n_tiles = rows // PROJ_ROWS
    qkv_w = 3 * DN_WIDTH
    rest_w = PROJ_WIDTH - qkv_w
    clamp = lambda t: jnp.clip(t, 0, n_tiles - 1)
    cur = lambda i: (clamp(i), 0)
    prev = lambda i: (clamp(i - 1), 0)
    prev_t = lambda i: (0, clamp(i - 1))
    prev2 = lambda i: (clamp(i - 2), 0)
    const = lambda i: (0, 0)
    qkv_spec = pl.BlockSpec((PROJ_ROWS, DN_WIDTH), prev2)
    qkv_sds = jax.ShapeDtypeStruct((rows, DN_WIDTH), BF16)
    return pl.pallas_call(
        functools.partial(_proj_conv_kernel, seq // PROJ_ROWS),
        grid=(n_tiles + 2,),
        in_specs=[
            pl.BlockSpec((PROJ_ROWS, D_MODEL), cur),
            pl.BlockSpec((1, D_MODEL), const),
            pl.BlockSpec((D_MODEL, PROJ_WIDTH), const),
            pl.BlockSpec((D_MODEL, LANES), const),
            pl.BlockSpec((2 * N_GATE, D_MODEL), const),
            pl.BlockSpec((SUBLANES, qkv_w), const),
        ],
        out_specs=[
            pl.BlockSpec((PROJ_ROWS, rest_w), prev),
            pl.BlockSpec((PROJ_ROWS, LANES), prev),
            pl.BlockSpec((2 * N_GATE, PROJ_ROWS), prev_t),
            qkv_spec, qkv_spec, qkv_spec,
        ],
        out_shape=[
            jax.ShapeDtypeStruct((rows, rest_w), BF16),
            jax.ShapeDtypeStruct((rows, LANES), F32),
            jax.ShapeDtypeStruct((2 * N_GATE, rows), F32),
            qkv_sds, qkv_sds, qkv_sds,
        ],
        scratch_shapes=[pltpu.VMEM((PROJ_ROWS + 2 * HALO, qkv_w), F32),
                        pltpu.VMEM((PROJ_ROWS, qkv_w), F32),
                        pltpu.VMEM((PROJ_ROWS, D_MODEL), BF16)],
        compiler_params=pltpu.CompilerParams(
            dimension_semantics=("arbitrary",), vmem_limit_bytes=VMEM_LIMIT),
        name="in_proj_conv",
    )(x2d, norm_w, w_perm, w_gate, w_gate_t, conv_w)


def _split3(x):
    x1 = x.astype(BF16)
    r1 = x - x1.astype(F32)
    x2 = r1.astype(BF16)
    x3 = (r1 - x2.astype(F32)).astype(BF16)
    return x1, x2, x3


def _exact_left(m_bf16, x):
    return sum(jnp.dot(m_bf16, p, preferred_element_type=F32) for p in _split3(x))


def _exact_right(x, m_bf16):
    return sum(jnp.dot(p, m_bf16, preferred_element_type=F32) for p in _split3(x))


def _tri_inverse_minus_eye(x0, row, col):
    same = lambda s: (row // s) == (col // s)
    s = 2
    p = jnp.where(same(s), x0, 0.0)
    while s < TILE:
        m_off = jnp.where(same(2 * s) & jnp.logical_not(same(s)), -x0, 0.0)
        z = m_off + _dot(m_off, p)
        yield
        p = p - z - _dot(p, z)
        yield
        s *= 2
    return p


def _dn_chain(q, k, v, beta, gcc, gcr, tot, incl, diag, row, col, state_ref, o_ref, hs):
    decay = jnp.where(incl, jnp.exp(jnp.where(incl, gcc - gcr, 0.0)), 0.0)
    gram = _dot_nt(k, k)
    qk = _dot_nt(q, k)
    yield
    x0 = jnp.where(diag, 0.0, -(gram * decay * beta))
    p = yield from _tri_inverse_minus_eye(x0, row, col)
    egc = jnp.exp(gcc)
    rhs = jnp.concatenate([v * beta, k * (beta * egc)], axis=1)
    uw = rhs + _dot(p, rhs)
    yield
    auw = _dot(qk * decay, uw)
    kt = _dot_tn(k * jnp.exp(tot - gcc), uw)
    yield
    p_q = q * egc - auw[:, TILE:]
    state = state_ref[...]
    res = _dot(jnp.concatenate([kt[:, TILE:], p_q], axis=0), state)
    yield
    o_ref[0, :, hs] = (res[TILE:] + auw[:, :TILE]).astype(BF16)
    state_ref[...] = jnp.exp(tot[0:1, :]) * state - res[:TILE] + kt[:, :TILE]


def _dn_kernel(qf_ref, kf_ref, vf_ref, gcf_ref, grf_ref,
               qb_ref, kb_ref, vb_ref, gcb_ref, grb_ref,
               alr_ref, dtr_ref, alc_ref, dtc_ref,
               of_ref, ob_ref, s_ref):
    @pl.when(pl.program_id(1) == 0)
    def _():
        s_ref[...] = jnp.zeros_like(s_ref)

    row = lax.broadcasted_iota(jnp.int32, (TILE, TILE), 0)
    col = lax.broadcasted_iota(jnp.int32, (TILE, TILE), 1)
    lower = (col <= row)
    upper = (col >= row)
    diag = (col == row)
    lower_bf = lower.astype(BF16)
    upper_bf = upper.astype(BF16)
    ones_bf = jnp.ones((TILE, TILE), BF16)

    streams = ((qf_ref, kf_ref, vf_ref, gcf_ref, grf_ref, of_ref, lower, lower_bf, upper_bf),
               (qb_ref, kb_ref, vb_ref, gcb_ref, grb_ref, ob_ref, upper, upper_bf, lower_bf))
    chains = []
    for d, (q_ref, k_ref, v_ref, gc_ref, gr_ref, o_ref, incl, cum_left, cum_right) in enumerate(streams):
        gate_c = gc_ref[0]
        gate_r = gr_ref[...]
        beta_c = jax.nn.sigmoid(gate_c)
        g_c = -jnp.exp(alr_ref[...]) * _softplus(gate_c + dtr_ref[...])
        g_r = -jnp.exp(alc_ref[...]) * _softplus(gate_r + dtc_ref[...])
        cum_c = _exact_left(cum_left, g_c)
        tot_c = _exact_left(ones_bf, g_c)
        cum_r = _exact_right(g_r, cum_right)
        for h in range(DN_HEADS):
            cb = d * DN_HEADS + h
            cg = N_GATE + cb
            hs = slice(h * LANES, (h + 1) * LANES)
            chains.append(_dn_chain(
                q_ref[0, :, hs].astype(F32), k_ref[0, :, hs].astype(F32), v_ref[0, :, hs].astype(F32),
                beta_c[:, cb:cb + 1], cum_c[:, cg:cg + 1], cum_r[cg:cg + 1, :], tot_c[:, cg:cg + 1],
                incl, diag, row, col, s_ref.at[cb], o_ref, hs))
    while chains:
        alive = []
        for chain in chains:
            try:
                next(chain)
                alive.append(chain)
            except StopIteration:
                pass
        chains = alive


def _dn_scan(q, k, v, gate_c, gate_t, al_row, dt_row, al_col, dt_col):
    b, s, _ = q.shape
    n_tiles = s // TILE
    fwd = lambda bb, i: (bb, i, 0)
    bwd = lambda bb, i: (bb, n_tiles - 1 - i, 0)
    qkv_f = pl.BlockSpec((1, TILE, DN_WIDTH), fwd)
    qkv_b = pl.BlockSpec((1, TILE, DN_WIDTH), bwd)
    gc_f = pl.BlockSpec((1, TILE, LANES), fwd)
    gc_b = pl.BlockSpec((1, TILE, LANES), bwd)
    gr_f = pl.BlockSpec((2 * N_GATE, TILE), lambda bb, i: (0, bb * n_tiles + i))
    gr_b = pl.BlockSpec((2 * N_GATE, TILE), lambda bb, i: (0, bb * n_tiles + n_tiles - 1 - i))
    row_p = pl.BlockSpec((1, LANES), lambda bb, i: (0, 0))
    col_p = pl.BlockSpec((2 * N_GATE, 1), lambda bb, i: (0, 0))
    out_sds = jax.ShapeDtypeStruct((b, s, DN_WIDTH), BF16)
    return pl.pallas_call(
        _dn_kernel,
        grid=(b, n_tiles),
        in_specs=[qkv_f, qkv_f, qkv_f, gc_f, gr_f,
                  qkv_b, qkv_b, qkv_b, gc_b, gr_b,
                  row_p, row_p, col_p, col_p],
        out_specs=[qkv_f, qkv_b],
        out_shape=[out_sds, out_sds],
        scratch_shapes=[pltpu.VMEM((N_GATE, DN_HEAD_DIM, DN_HEAD_DIM), F32)],
        compiler_params=pltpu.CompilerParams(
            dimension_semantics=("arbitrary", "arbitrary"), vmem_limit_bytes=VMEM_LIMIT),
        name="dn_scan",
    )(q, k, v, gate_c, gate_t, q, k, v, gate_c, gate_t, al_row, dt_row, al_col, dt_col)


def _pair_rms(x, w, head_ones):
    x2 = x * x
    hi = x2.astype(BF16)
    lo = (x2 - hi.astype(F32)).astype(BF16)
    ss = (jnp.dot(hi, head_ones, preferred_element_type=F32)
          + jnp.dot(lo, head_ones, preferred_element_type=F32))
    return (x * lax.rsqrt(ss * (1.0 / SWA_HEAD_DIM) + EPS)) * w


def _pair_rope(x, cos, sin_signed, first_quarter):
    half = SWA_HEAD_DIM // 2
    rot = jnp.where(first_quarter, pltpu.roll(x, LANES - half, 1), pltpu.roll(x, half, 1))
    return x * cos + rot * sin_signed


def _swa_head(q_m, k_span, v_span, valid, sink, o_ref, lanes, needs_roll):
    sc = lax.dot_general(q_m, k_span, NT_DIMS, preferred_element_type=F32)
    yield
    sc = jnp.where(valid, sc, -1e30)
    mx = jnp.maximum(jnp.max(sc, axis=-1, keepdims=True), sink)
    yield
    p = jnp.exp(sc - mx)
    denom = jnp.sum(p, axis=-1, keepdims=True) + jnp.exp(sink - mx)
    yield
    pv = jnp.dot(p.astype(BF16), v_span, preferred_element_type=F32) / denom
    if needs_roll:
        pv = pltpu.roll(pv, SWA_HEAD_DIM, 1)
    o_ref[0, :, lanes] = pv[:, lanes.start % LANES:(lanes.start % LANES) + SWA_HEAD_DIM].astype(BF16)


def _swa_kernel(q_ref, kc_ref, kn_ref, vp_ref, vc_ref, vn_ref,
                cc_ref, sc_ref, cn_ref, sn_ref,
                qw_ref, kw_ref, sink_ref, o_ref, ks_ref):
    i = pl.program_id(1)
    n = pl.num_programs(1)
    lane = lax.broadcasted_iota(jnp.int32, (TILE, LANES), 1)
    lo_half = lane < SWA_HEAD_DIM
    first_quarter = (lane % SWA_HEAD_DIM) < (SWA_HEAD_DIM // 2)

    sub = lax.broadcasted_iota(jnp.int32, (LANES, LANES), 0)
    head_ones = ((sub // SWA_HEAD_DIM) == (lane // SWA_HEAD_DIM)).astype(BF16)

    def prep(x, w, cos_ref, sin_ref):
        return _pair_rope(_pair_rms(x, w, head_ones), cos_ref[...], sin_ref[...], first_quarter)

    kw = kw_ref[...]

    @pl.when(i == 0)
    def _():
        ks_ref[0] = jnp.zeros((TILE, LANES), BF16)
        ks_ref[1] = prep(kc_ref[0].astype(F32), kw, cc_ref, sc_ref).astype(BF16)

    @pl.when(i > 0)
    def _():
        ks_ref[0] = ks_ref[1]
        ks_ref[1] = ks_ref[2]

    ks_ref[2] = prep(kn_ref[0].astype(F32), kw, cn_ref, sn_ref).astype(BF16)
    k_span = ks_ref[...].reshape(3 * TILE, LANES)
    v_span = jnp.concatenate([vp_ref[0], vc_ref[0], vn_ref[0]], axis=0)
    span = 3 * TILE
    r = lax.broadcasted_iota(jnp.int32, (TILE, span), 0)
    c = lax.broadcasted_iota(jnp.int32, (TILE, span), 1)
    valid = (jnp.abs(c - WINDOW - r) <= WINDOW)
    valid = valid & ((c >= TILE) | (i > 0)) & ((c < 2 * TILE) | (i < n - 1))
    group = SWA_Q_HEADS // SWA_KV_HEADS
    qw = qw_ref[...]
    scale = SWA_HEAD_DIM ** -0.5
    heads = []
    for j in range(SWA_Q_HEADS // 2):
        q_pair = prep(q_ref[0, :, j * LANES:(j + 1) * LANES].astype(F32), qw, cc_ref, sc_ref) * scale
        q_swap = pltpu.roll(q_pair, SWA_HEAD_DIM, 1)
        for half in range(2):
            hq = 2 * j + half
            g = hq // group
            q_al = q_pair if half == g else q_swap
            q_m = jnp.where(lo_half if g == 0 else jnp.logical_not(lo_half), q_al, 0.0).astype(BF16)
            lanes = slice(hq * SWA_HEAD_DIM, (hq + 1) * SWA_HEAD_DIM)
            heads.append(_swa_head(q_m, k_span, v_span, valid, sink_ref[hq], o_ref, lanes, half != g))
    while heads:
        alive = []
        for head in heads:
            try:
                next(head)
                alive.append(head)
            except StopIteration:
                pass
        heads = alive


def _swa(proj3d, cos_t, sin_t, qw, kw, sinks):
    b, s, _ = proj3d.shape
    n_tiles = s // TILE
    kblk = COL_SWK // LANES
    vblk = COL_SWV // LANES
    prev = lambda i: jnp.maximum(i - 1, 0)
    nxt = lambda i: jnp.minimum(i + 1, n_tiles - 1)

    def kv_spec(blk, f):
        return pl.BlockSpec((1, TILE, LANES), lambda bb, i: (bb, f(i), blk))

    def tab_spec(f):
        return pl.BlockSpec((TILE, LANES), lambda bb, i: (f(i), 0))

    ident = lambda i: i
    row_p = pl.BlockSpec((1, LANES), lambda bb, i: (0, 0))
    return pl.pallas_call(
        _swa_kernel,
        grid=(b, n_tiles),
        in_specs=[
            pl.BlockSpec((1, TILE, SWA_WIDTH), lambda bb, i: (bb, i, COL_SWQ // SWA_WIDTH)),
            kv_spec(kblk, ident), kv_spec(kblk, nxt),
            kv_spec(vblk, prev), kv_spec(vblk, ident), kv_spec(vblk, nxt),
            tab_spec(ident), tab_spec(ident), tab_spec(nxt), tab_spec(nxt),
            row_p, row_p,
            pl.BlockSpec(memory_space=pltpu.SMEM),
        ],
        out_specs=pl.BlockSpec((1, TILE, SWA_WIDTH), lambda bb, i: (bb, i, 0)),
        out_shape=jax.ShapeDtypeStruct((b, s, SWA_WIDTH), BF16),
        scratch_shapes=[pltpu.VMEM((3, TILE, LANES), BF16)],
        compiler_params=pltpu.CompilerParams(
            dimension_semantics=("arbitrary", "arbitrary"), vmem_limit_bytes=VMEM_LIMIT),
        name="swa",
    )(proj3d, proj3d, proj3d, proj3d, proj3d, proj3d,
      cos_t, sin_t, cos_t, sin_t, qw, kw, sinks)


def _out_kernel(of_ref, ob_ref, z_ref, sw_ref, swz_ref, x_ref, onw_ref, wo_ref, y_ref):
    dn = of_ref[...].astype(F32) + ob_ref[...].astype(F32)
    onw = onw_ref[...]
    parts = []
    for h in range(DN_HEADS):
        blk = dn[:, h * LANES:(h + 1) * LANES]
        ms = jnp.mean(blk * blk, axis=-1, keepdims=True)
        parts.append((blk * lax.rsqrt(ms + EPS)) * onw)
    dn_n = jnp.concatenate(parts, axis=1) * _silu(z_ref[...].astype(F32))
    sw = sw_ref[...].astype(F32) * _silu(swz_ref[...].astype(F32))
    mix = jnp.concatenate([dn_n, sw], axis=1).astype(BF16)
    y_ref[...] = x_ref[...] + jnp.dot(mix, wo_ref[...], preferred_element_type=F32)


def _out_projection(o_f, o_b, proj2d, sw, x2d, out_norm_w, w_out):
    rows = x2d.shape[0]
    half = lambda i: (i, 0)
    return pl.pallas_call(
        _out_kernel,
        grid=(rows // PROJ_ROWS,),
        in_specs=[
            pl.BlockSpec((PROJ_ROWS, DN_WIDTH), half),
            pl.BlockSpec((PROJ_ROWS, DN_WIDTH), half),
            pl.BlockSpec((PROJ_ROWS, DN_WIDTH), lambda i: (i, COL_DNZ // DN_WIDTH)),
            pl.BlockSpec((PROJ_ROWS, SWA_WIDTH), half),
            pl.BlockSpec((PROJ_ROWS, SWA_WIDTH), lambda i: (i, COL_SWZ // SWA_WIDTH)),
            pl.BlockSpec((PROJ_ROWS, D_MODEL), half),
            pl.BlockSpec((1, LANES), lambda i: (0, 0)),
            pl.BlockSpec((DN_WIDTH + SWA_WIDTH, D_MODEL), lambda i: (0, 0)),
        ],
        out_specs=pl.BlockSpec((PROJ_ROWS, D_MODEL), half),
        out_shape=jax.ShapeDtypeStruct((rows, D_MODEL), F32),
        compiler_params=pltpu.CompilerParams(
            dimension_semantics=("arbitrary",), vmem_limit_bytes=VMEM_LIMIT),
        name="out_proj",
    )(o_f, o_b, proj2d, sw, proj2d, x2d, out_norm_w, w_out)


def _rope_tables(seq):
    inv_freq = ROPE_THETA ** (-jnp.arange(0, SWA_HEAD_DIM, 2, dtype=F32) / SWA_HEAD_DIM)
    ang = jnp.arange(seq, dtype=F32)[:, None] * inv_freq[None, :]
    ang = jnp.concatenate([ang, ang], axis=-1)
    cos, sin = jnp.cos(ang), jnp.sin(ang)
    half = SWA_HEAD_DIM // 2
    sin_signed = jnp.concatenate([-sin[:, :half], sin[:, half:]], axis=-1)
    reps = LANES // SWA_HEAD_DIM
    return jnp.tile(cos, (1, reps)), jnp.tile(sin_signed, (1, reps))


def _layer(x, norm_w, w_in, conv_w, a_log, dt_bias, out_norm_w, q_norm_w, k_norm_w, sinks, w_out):
    b, s, _ = x.shape
    x2d = x.reshape(b * s, D_MODEL)
    o_q, o_k, o_v, o_z = 0, DN_WIDTH, 2 * DN_WIDTH, 3 * DN_WIDTH
    o_beta = 4 * DN_WIDTH
    o_decay = o_beta + N_GATE
    o_swq = o_decay + N_GATE
    o_swk = o_swq + SWA_WIDTH
    o_swv = o_swk + SWA_KV_WIDTH
    o_swz = o_swv + SWA_KV_WIDTH
    w_gate = w_in[:, o_beta:o_swq]
    w_perm = jnp.concatenate([
        w_in[:, o_q:o_beta],
        w_in[:, o_swq:o_swk], w_in[:, o_swz:o_swz + SWA_WIDTH],
        w_in[:, o_swk:o_swv], w_in[:, o_swv:o_swz]], axis=1).astype(BF16)
    w_gate_pad = jnp.concatenate(
        [w_gate, jnp.zeros((D_MODEL, LANES - 2 * N_GATE), w_in.dtype)], axis=1).astype(BF16)
    conv_pad = jnp.concatenate([conv_w, jnp.zeros((SUBLANES - DN_CONV, 3 * DN_WIDTH), conv_w.dtype)], axis=0)
    proj2d, gate_c, gate_t, q, k, v = _in_projection_conv(
        x2d, s, norm_w.reshape(1, D_MODEL), w_perm, w_gate_pad, w_gate.T.astype(BF16), conv_pad)
    proj3d = proj2d.reshape(b, s, PROJ_WIDTH - 3 * DN_WIDTH)
    q, k, v = (a.reshape(b, s, DN_WIDTH) for a in (q, k, v))

    a_flat = a_log.reshape(N_GATE).astype(F32)
    d_flat = dt_bias.reshape(N_GATE).astype(F32)
    zeros8 = jnp.zeros((N_GATE,), F32)
    pad_row = jnp.zeros((LANES - 2 * N_GATE,), F32)
    al_row = jnp.concatenate([zeros8, a_flat, pad_row]).reshape(1, LANES)
    dt_row = jnp.concatenate([zeros8, d_flat, pad_row]).reshape(1, LANES)
    al_col = jnp.concatenate([zeros8, a_flat]).reshape(2 * N_GATE, 1)
    dt_col = jnp.concatenate([zeros8, d_flat]).reshape(2 * N_GATE, 1)
    o_f, o_b = _dn_scan(q, k, v, gate_c.reshape(b, s, LANES), gate_t, al_row, dt_row, al_col, dt_col)

    cos_t, sin_t = _rope_tables(s)
    reps = LANES // SWA_HEAD_DIM
    sw = _swa(proj3d, cos_t, sin_t,
              jnp.tile(q_norm_w.astype(F32), reps).reshape(1, LANES),
              jnp.tile(k_norm_w.astype(F32), reps).reshape(1, LANES),
              sinks.astype(F32))

    y = _out_projection(o_f.reshape(b * s, DN_WIDTH), o_b.reshape(b * s, DN_WIDTH), proj2d,
                        sw.reshape(b * s, SWA_WIDTH), x2d,
                        out_norm_w.reshape(1, LANES).astype(F32), w_out.astype(BF16))
    return y.reshape(b, s, D_MODEL)


def kernel(x, norm_w, w_in, dn_conv_w, dn_a_log, dn_dt_bias, dn_out_norm_w,
           swa_q_norm_w, swa_k_norm_w, swa_sinks, w_out):
    for l in range(norm_w.shape[0]):
        x = _layer(x, norm_w[l], w_in[l], dn_conv_w[l], dn_a_log[l], dn_dt_bias[l], dn_out_norm_w[l],
                   swa_q_norm_w[l], swa_k_norm_w[l], swa_sinks[l], w_out[l])
    return x
```

```python
import functools

import jax
import jax.numpy as jnp
from jax import lax
from jax.experimental import pallas as pl
from jax.experimental.pallas import tpu as pltpu

D_MODEL = 1024
DN_HEADS = 4
DN_HEAD_DIM = 128
DN_WIDTH = DN_HEADS * DN_HEAD_DIM
DN_CONV = 5
N_DIR = 2
N_GATE = N_DIR * DN_HEADS
SWA_Q_HEADS = 8
SWA_KV_HEADS = 2
SWA_HEAD_DIM = 64
SWA_WIDTH = SWA_Q_HEADS * SWA_HEAD_DIM
SWA_KV_WIDTH = SWA_KV_HEADS * SWA_HEAD_DIM
WINDOW = 128
ROPE_THETA = 10000.0
EPS = 1e-6

LANES = 128
SUBLANES = 8
TILE = 128
PROJ_ROWS = 512
VMEM_LIMIT = 48 * 1024 * 1024

COL_DNZ = 0
COL_SWQ = COL_DNZ + DN_WIDTH
COL_SWZ = COL_SWQ + SWA_WIDTH
COL_SWK = COL_SWZ + SWA_WIDTH
COL_SWV = COL_SWK + SWA_KV_WIDTH
PROJ_WIDTH = 3 * DN_WIDTH + COL_SWV + SWA_KV_WIDTH
HALO = 16

F32 = jnp.float32
BF16 = jnp.bfloat16
NT_DIMS = (((1,), (1,)), ((), ()))
TN_DIMS = (((0,), (0,)), ((), ()))


def _silu(x):
    return x * jax.nn.sigmoid(x)


def _softplus(x):
    return jnp.maximum(x, 0.0) + jnp.log1p(jnp.exp(-jnp.abs(x)))


def _dot(a, b):
    return jnp.dot(a.astype(BF16), b.astype(BF16), preferred_element_type=F32)


def _dot_nt(a, b):
    return lax.dot_general(a.astype(BF16), b.astype(BF16), NT_DIMS, preferred_element_type=F32)


def _dot_tn(a, b):
    return lax.dot_general(a.astype(BF16), b.astype(BF16), TN_DIMS, preferred_element_type=F32)


def _proj_conv_kernel(tiles_per_seq, x_ref, nw_ref, w_ref, wg_ref, wgt_ref, cw_ref,
                      rest_ref, gc_ref, gt_ref, q_ref, k_ref, v_ref, buf_ref, new_ref, h_ref):
    i = pl.program_id(0)
    n_tiles = pl.num_programs(0) - 2
    rows = x_ref.shape[0]
    pad = (DN_CONV - 1) // 2
    qkv_w = 3 * DN_WIDTH

    @pl.when(i == 0)
    def _():
        buf_ref[...] = jnp.zeros_like(buf_ref)
        h_ref[...] = jnp.zeros_like(h_ref)

    h = h_ref[...]
    t = i - 1
    boundary = (t % tiles_per_seq == 0) | (t < 0) | (t >= n_tiles)
    head = jnp.dot(h[0:HALO, :], w_ref[:, 0:qkv_w], preferred_element_type=F32)
    buf_ref[HALO + rows:2 * HALO + rows, :] = jnp.where(boundary, 0.0, head)
    new_ref[...] = jnp.dot(h, w_ref[:, 0:qkv_w], preferred_element_type=F32)

    gc_ref[...] = jnp.dot(h, wg_ref[...], preferred_element_type=F32)
    gt_ref[...] = lax.dot_general(wgt_ref[...], h, NT_DIMS, preferred_element_type=F32)

    outs = (q_ref, k_ref, v_ref)
    rest_w = rest_ref.shape[1]
    chunk = 2 * LANES
    n_chunks = rest_w // chunk
    n_blocks = 3 * DN_HEADS
    for c in range(n_blocks):
        for r in range(c * n_chunks // n_blocks, (c + 1) * n_chunks // n_blocks):
            cols = slice(r * chunk, (r + 1) * chunk)
            rest_ref[:, cols] = jnp.dot(h, w_ref[:, qkv_w + r * chunk:qkv_w + (r + 1) * chunk],
                                        preferred_element_type=F32).astype(BF16)
        lo = c * LANES
        acc = None
        for j in range(DN_CONV):
            r0 = HALO - pad + j
            term = buf_ref[r0:r0 + rows, lo:lo + LANES] * cw_ref[j:j + 1, lo:lo + LANES]
            acc = term if acc is None else acc + term
        y = _silu(acc)
        part, head = divmod(c, DN_HEADS)
        if part < 2:
            y = y * lax.rsqrt(jnp.sum(y * y, axis=-1, keepdims=True) + EPS)
        if part == 0:
            y = y * (DN_HEAD_DIM ** -0.5)
        outs[part][:, head * LANES:(head + 1) * LANES] = y.astype(BF16)

    buf_ref[0:HALO, :] = jnp.where(boundary, 0.0, buf_ref[rows:rows + HALO, :])
    buf_ref[HALO:HALO + rows, :] = new_ref[...]

    x = x_ref[...]
    ms = jnp.mean(x * x, axis=-1, keepdims=True)
    h_ref[...] = ((x * lax.rsqrt(ms + EPS)) * nw_ref[...]).astype(BF16)


def _in_projection_conv(x2d, seq, norm_w, w_perm, w_gate, w_gate_t, conv_w):
    rows = x2d.shape[0]
    n_tiles = rows // PROJ_ROWS
    qkv_w = 3 * DN_WIDTH
    rest_w = PROJ_WIDTH - qkv_w
    clamp = lambda t: jnp.clip(t, 0, n_tiles - 1)
    cur = lambda i: (clamp(i), 0)
    prev = lambda i: (clamp(i - 1), 0)
    prev_t = lambda i: (0, clamp(i - 1))
    prev2 = lambda i: (clamp(i - 2), 0)
    const = lambda i: (0, 0)
    qkv_spec = pl.BlockSpec((PROJ_ROWS, DN_WIDTH), prev2)
    qkv_sds = jax.ShapeDtypeStruct((rows, DN_WIDTH), BF16)
    return pl.pallas_call(
        functools.partial(_proj_conv_kernel, seq // PROJ_ROWS),
        grid=(n_tiles + 2,),
        in_specs=[
            pl.BlockSpec((PROJ_ROWS, D_MODEL), cur),
            pl.BlockSpec((1, D_MODEL), const),
            pl.BlockSpec((D_MODEL, PROJ_WIDTH), const),
            pl.BlockSpec((D_MODEL, LANES), const),
            pl.BlockSpec((2 * N_GATE, D_MODEL), const),
            pl.BlockSpec((SUBLANES, qkv_w), const),
        ],
        out_specs=[
            pl.BlockSpec((PROJ_ROWS, rest_w), prev),
            pl.BlockSpec((PROJ_ROWS, LANES), prev),
            pl.BlockSpec((2 * N_GATE, PROJ_ROWS), prev_t),
            qkv_spec, qkv_spec, qkv_spec,
        ],
        out_shape=[
            jax.ShapeDtypeStruct((rows, rest_w), BF16),
            jax.ShapeDtypeStruct((rows, LANES), F32),
            jax.ShapeDtypeStruct((2 * N_GATE, rows), F32),
            qkv_sds, qkv_sds, qkv_sds,
        ],
        scratch_shapes=[pltpu.VMEM((PROJ_ROWS + 2 * HALO, qkv_w), F32),
                        pltpu.VMEM((PROJ_ROWS, qkv_w), F32),
                        pltpu.VMEM((PROJ_ROWS, D_MODEL), BF16)],
        compiler_params=pltpu.CompilerParams(
            dimension_semantics=("arbitrary",), vmem_limit_bytes=VMEM_LIMIT),
        name="in_proj_conv",
    )(x2d, norm_w, w_perm, w_gate, w_gate_t, conv_w)


def _split3(x):
    x1 = x.astype(BF16)
    r1 = x - x1.astype(F32)
    x2 = r1.astype(BF16)
    x3 = (r1 - x2.astype(F32)).astype(BF16)
    return x1, x2, x3


def _exact_left(m_bf16, x):
    return sum(jnp.dot(m_bf16, p, preferred_element_type=F32) for p in _split3(x))


def _exact_right(x, m_bf16):
    return sum(jnp.dot(p, m_bf16, preferred_element_type=F32) for p in _split3(x))


def _tri_inverse_minus_eye(x0, row, col, lower):
    same = lambda s: (row // s) == (col // s)
    s = 2
    p = jnp.where(same(s), x0, 0.0)
    while s < TILE:
        m_off = jnp.where(same(2 * s) & jnp.logical_not(same(s)), -x0, 0.0)
        if s < SUBLANES:
            z = m_off + _dot(m_off, p)
            yield
            p = p - z - _dot(p, z)
            yield
        else:
            starts = range(s if lower else 0, TILE, 2 * s)
            take = lambda a: jnp.concatenate([a[r:r + s] for r in starts], axis=0)

            def put(rows, base):
                parts = [rows[(r // (2 * s)) * s:(r // (2 * s) + 1) * s] if r in starts else base[r:r + s]
                         for r in range(0, TILE, s)]
                return jnp.concatenate(parts, axis=0)

            m_rows = take(m_off)
            z_rows = m_rows + _dot(m_rows, p)
            yield
            new_rows = take(p) - z_rows - _dot(take(p), put(z_rows, jnp.zeros_like(p)))
            yield
            p = put(new_rows, p)
        s *= 2
    return p


def _dn_chain(q, k, v, beta, gcc, gcr, tot, incl, lower, diag, row, col, state_ref, o_ref, hs):
    decay = jnp.where(incl, jnp.exp(jnp.where(incl, gcc - gcr, 0.0)), 0.0)
    gram = _dot_nt(k, k)
    qk = _dot_nt(q, k)
    yield
    x0 = jnp.where(diag, 0.0, -(gram * decay * beta))
    p = yield from _tri_inverse_minus_eye(x0, row, col, lower)
    egc = jnp.exp(gcc)
    rhs = jnp.concatenate([v * beta, k * (beta * egc)], axis=1)
    uw = rhs + _dot(p, rhs)
    yield
    auw = _dot(qk * decay, uw)
    kt = _dot_tn(k * jnp.exp(tot - gcc), uw)
    yield
    p_q = q * egc - auw[:, TILE:]
    state = state_ref[...]
    res = _dot(jnp.concatenate([kt[:, TILE:], p_q], axis=0), state)
    yield
    o_ref[:, hs] = (res[TILE:] + auw[:, :TILE]).astype(BF16)
    state_ref[...] = jnp.exp(tot[0:1, :]) * state - res[:TILE] + kt[:, :TILE]


def _dn_kernel(batch, *refs):
    n_in = 4 + batch
    fwd_refs, bwd_refs = refs[0:n_in], refs[n_in:2 * n_in]
    alr_ref, dtr_ref, alc_ref, dtc_ref, of_ref, ob_ref, s_ref = refs[2 * n_in:]

    @pl.when(pl.program_id(0) == 0)
    def _():
        s_ref[...] = jnp.zeros_like(s_ref)

    row = lax.broadcasted_iota(jnp.int32, (TILE, TILE), 0)
    col = lax.broadcasted_iota(jnp.int32, (TILE, TILE), 1)
    lower = (col <= row)
    upper = (col >= row)
    diag = (col == row)
    lower_bf = lower.astype(BF16)
    upper_bf = upper.astype(BF16)
    ones_bf = jnp.ones((TILE, TILE), BF16)
    neg_a_row = -jnp.exp(alr_ref[...])
    neg_a_col = -jnp.exp(alc_ref[...])

    streams = ((fwd_refs, of_ref, lower, lower_bf, upper_bf),
               (bwd_refs, ob_ref, upper, upper_bf, lower_bf))
    chains = []
    for d, (in_refs, o_ref, incl, cum_left, cum_right) in enumerate(streams):
        q_ref, k_ref, v_ref, gc_ref = in_refs[0:4]
        for b in range(batch):
            gate_c = gc_ref[b]
            gate_r = in_refs[4 + b][...]
            beta_c = jax.nn.sigmoid(gate_c)
            g_c = neg_a_row * _softplus(gate_c + dtr_ref[...])
            g_r = neg_a_col * _softplus(gate_r + dtc_ref[...])
            cum_c = _exact_left(cum_left, g_c)
            tot_c = _exact_left(ones_bf, g_c)
            cum_r = _exact_right(g_r, cum_right)
            for h in range(DN_HEADS):
                cb = d * DN_HEADS + h
                cg = N_GATE + cb
                hs = slice(h * LANES, (h + 1) * LANES)
                chains.append(_dn_chain(
                    q_ref[b, :, hs].astype(F32), k_ref[b, :, hs].astype(F32), v_ref[b, :, hs].astype(F32),
                    beta_c[:, cb:cb + 1], cum_c[:, cg:cg + 1], cum_r[cg:cg + 1, :], tot_c[:, cg:cg + 1],
                    incl, d == 0, diag, row, col, s_ref.at[b * N_GATE + cb], o_ref.at[b], hs))
    while chains:
        alive = []
        for chain in chains:
            try:
                next(chain)
                alive.append(chain)
            except StopIteration:
                pass
        chains = alive


def _dn_scan(q, k, v, gate_c, gate_t, al_row, dt_row, al_col, dt_col):
    b, s, _ = q.shape
    n_tiles = s // TILE
    fwd = lambda i: (0, i, 0)
    bwd = lambda i: (0, n_tiles - 1 - i, 0)
    const = lambda i: (0, 0)
    qkv_f = pl.BlockSpec((b, TILE, DN_WIDTH), fwd)
    qkv_b = pl.BlockSpec((b, TILE, DN_WIDTH), bwd)
    gc_f = pl.BlockSpec((b, TILE, LANES), fwd)
    gc_b = pl.BlockSpec((b, TILE, LANES), bwd)
    gr_f = [pl.BlockSpec((2 * N_GATE, TILE), functools.partial(lambda bb, i: (0, bb * n_tiles + i), bb))
            for bb in range(b)]
    gr_b = [pl.BlockSpec((2 * N_GATE, TILE),
                         functools.partial(lambda bb, i: (0, bb * n_tiles + n_tiles - 1 - i), bb))
            for bb in range(b)]
    row_p = pl.BlockSpec((1, LANES), const)
    col_p = pl.BlockSpec((2 * N_GATE, 1), const)
    out_sds = jax.ShapeDtypeStruct((b, s, DN_WIDTH), BF16)
    return pl.pallas_call(
        functools.partial(_dn_kernel, b),
        grid=(n_tiles,),
        in_specs=[qkv_f, qkv_f, qkv_f, gc_f, *gr_f,
                  qkv_b, qkv_b, qkv_b, gc_b, *gr_b,
                  row_p, row_p, col_p, col_p],
        out_specs=[qkv_f, qkv_b],
        out_shape=[out_sds, out_sds],
        scratch_shapes=[pltpu.VMEM((b * N_GATE, DN_HEAD_DIM, DN_HEAD_DIM), F32)],
        compiler_params=pltpu.CompilerParams(
            dimension_semantics=("arbitrary",), vmem_limit_bytes=VMEM_LIMIT),
        name="dn_scan",
    )(q, k, v, gate_c, *([gate_t] * b), q, k, v, gate_c, *([gate_t] * b), al_row, dt_row, al_col, dt_col)


def _pair_rms(x, w, head_ones):
    x2 = x * x
    hi = x2.astype(BF16)
    lo = (x2 - hi.astype(F32)).astype(BF16)
    ss = (jnp.dot(hi, head_ones, preferred_element_type=F32)
          + jnp.dot(lo, head_ones, preferred_element_type=F32))
    return (x * lax.rsqrt(ss * (1.0 / SWA_HEAD_DIM) + EPS)) * w


def _pair_rope(x, cos, sin_signed, first_quarter):
    half = SWA_HEAD_DIM // 2
    rot = jnp.where(first_quarter, pltpu.roll(x, LANES - half, 1), pltpu.roll(x, half, 1))
    return x * cos + rot * sin_signed


def _swa_head(q_m, k_span, v_span, valid, sink, o_ref, lanes, needs_roll):
    sc = lax.dot_general(q_m, k_span, NT_DIMS, preferred_element_type=F32)
    yield
    sc = jnp.where(valid, sc, -1e30)
    mx = jnp.maximum(jnp.max(sc, axis=-1, keepdims=True), sink)
    yield
    p = jnp.exp(sc - mx)
    denom = jnp.sum(p, axis=-1, keepdims=True) + jnp.exp(sink - mx)
    yield
    pv = jnp.dot(p.astype(BF16), v_span, preferred_element_type=F32) / denom
    if needs_roll:
        pv = pltpu.roll(pv, SWA_HEAD_DIM, 1)
    o_ref[0, :, lanes] = pv[:, lanes.start % LANES:(lanes.start % LANES) + SWA_HEAD_DIM].astype(BF16)


def _swa_kernel(q_ref, kc_ref, kn_ref, vp_ref, vc_ref, vn_ref,
                cc_ref, sc_ref, cn_ref, sn_ref,
                qw_ref, kw_ref, sink_ref, o_ref, ks_ref):
    i = pl.program_id(1)
    n = pl.num_programs(1)
    lane = lax.broadcasted_iota(jnp.int32, (TILE, LANES), 1)
    lo_half = lane < SWA_HEAD_DIM
    first_quarter = (lane % SWA_HEAD_DIM) < (SWA_HEAD_DIM // 2)

    sub = lax.broadcasted_iota(jnp.int32, (LANES, LANES), 0)
    head_ones = ((sub // SWA_HEAD_DIM) == (lane // SWA_HEAD_DIM)).astype(BF16)

    def prep(x, w, cos_ref, sin_ref):
        return _pair_rope(_pair_rms(x, w, head_ones), cos_ref[...], sin_ref[...], first_quarter)

    kw = kw_ref[...]

    @pl.when(i == 0)
    def _():
        ks_ref[0] = jnp.zeros((TILE, LANES), BF16)
        ks_ref[1] = prep(kc_ref[0].astype(F32), kw, cc_ref, sc_ref).astype(BF16)

    @pl.when(i > 0)
    def _():
        ks_ref[0] = ks_ref[1]
        ks_ref[1] = ks_ref[2]

    ks_ref[2] = prep(kn_ref[0].astype(F32), kw, cn_ref, sn_ref).astype(BF16)
    k_span = ks_ref[...].reshape(3 * TILE, LANES)
    v_span = jnp.concatenate([vp_ref[0], vc_ref[0], vn_ref[0]], axis=0)
    span = 3 * TILE
    r = lax.broadcasted_iota(jnp.int32, (TILE, span), 0)
    c = lax.broadcasted_iota(jnp.int32, (TILE, span), 1)
    valid = (jnp.abs(c - WINDOW - r) <= WINDOW)
    valid = valid & ((c >= TILE) | (i > 0)) & ((c < 2 * TILE) | (i < n - 1))
    group = SWA_Q_HEADS // SWA_KV_HEADS
    qw = qw_ref[...]
    scale = SWA_HEAD_DIM ** -0.5
    heads = []
    for j in range(SWA_Q_HEADS // 2):
        q_pair = prep(q_ref[0, :, j * LANES:(j + 1) * LANES].astype(F32), qw, cc_ref, sc_ref) * scale
        q_swap = pltpu.roll(q_pair, SWA_HEAD_DIM, 1)
        for half in range(2):
            hq = 2 * j + half
            g = hq // group
            q_al = q_pair if half == g else q_swap
            q_m = jnp.where(lo_half if g == 0 else jnp.logical_not(lo_half), q_al, 0.0).astype(BF16)
            lanes = slice(hq * SWA_HEAD_DIM, (hq + 1) * SWA_HEAD_DIM)
            heads.append(_swa_head(q_m, k_span, v_span, valid, sink_ref[hq], o_ref, lanes, half != g))
    while heads:
        alive = []
        for head in heads:
            try:
                next(head)
                alive.append(head)
            except StopIteration:
                pass
        heads = alive


def _swa(proj3d, cos_t, sin_t, qw, kw, sinks):
    b, s, _ = proj3d.shape
    n_tiles = s // TILE
    kblk = COL_SWK // LANES
    vblk = COL_SWV // LANES
    prev = lambda i: jnp.maximum(i - 1, 0)
    nxt = lambda i: jnp.minimum(i + 1, n_tiles - 1)

    def kv_spec(blk, f):
        return pl.BlockSpec((1, TILE, LANES), lambda bb, i: (bb, f(i), blk))

    def tab_spec(f):
        return pl.BlockSpec((TILE, LANES), lambda bb, i: (f(i), 0))

    ident = lambda i: i
    row_p = pl.BlockSpec((1, LANES), lambda bb, i: (0, 0))
    return pl.pallas_call(
        _swa_kernel,
        grid=(b, n_tiles),
        in_specs=[
            pl.BlockSpec((1, TILE, SWA_WIDTH), lambda bb, i: (bb, i, COL_SWQ // SWA_WIDTH)),
            kv_spec(kblk, ident), kv_spec(kblk, nxt),
            kv_spec(vblk, prev), kv_spec(vblk, ident), kv_spec(vblk, nxt),
            tab_spec(ident), tab_spec(ident), tab_spec(nxt), tab_spec(nxt),
            row_p, row_p,
            pl.BlockSpec(memory_space=pltpu.SMEM),
        ],
        out_specs=pl.BlockSpec((1, TILE, SWA_WIDTH), lambda bb, i: (bb, i, 0)),
        out_shape=jax.ShapeDtypeStruct((b, s, SWA_WIDTH), BF16),
        scratch_shapes=[pltpu.VMEM((3, TILE, LANES), BF16)],
        compiler_params=pltpu.CompilerParams(
            dimension_semantics=("arbitrary", "arbitrary"), vmem_limit_bytes=VMEM_LIMIT),
        name="swa",
    )(proj3d, proj3d, proj3d, proj3d, proj3d, proj3d,
      cos_t, sin_t, cos_t, sin_t, qw, kw, sinks)


def _out_kernel(of_ref, ob_ref, z_ref, sw_ref, swz_ref, x_ref, onw_ref, wo_ref, y_ref):
    dn = of_ref[...].astype(F32) + ob_ref[...].astype(F32)
    onw = onw_ref[...]
    parts = []
    for h in range(DN_HEADS):
        blk = dn[:, h * LANES:(h + 1) * LANES]
        ms = jnp.mean(blk * blk, axis=-1, keepdims=True)
        parts.append((blk * lax.rsqrt(ms + EPS)) * onw)
    dn_n = jnp.concatenate(parts, axis=1) * _silu(z_ref[...].astype(F32))
    sw = sw_ref[...].astype(F32) * _silu(swz_ref[...].astype(F32))
    mix = jnp.concatenate([dn_n, sw], axis=1).astype(BF16)
    y_ref[...] = x_ref[...] + jnp.dot(mix, wo_ref[...], preferred_element_type=F32)


def _out_projection(o_f, o_b, proj2d, sw, x2d, out_norm_w, w_out):
    rows = x2d.shape[0]
    half = lambda i: (i, 0)
    return pl.pallas_call(
        _out_kernel,
        grid=(rows // PROJ_ROWS,),
        in_specs=[
            pl.BlockSpec((PROJ_ROWS, DN_WIDTH), half),
            pl.BlockSpec((PROJ_ROWS, DN_WIDTH), half),
            pl.BlockSpec((PROJ_ROWS, DN_WIDTH), lambda i: (i, COL_DNZ // DN_WIDTH)),
            pl.BlockSpec((PROJ_ROWS, SWA_WIDTH), half),
            pl.BlockSpec((PROJ_ROWS, SWA_WIDTH), lambda i: (i, COL_SWZ // SWA_WIDTH)),
            pl.BlockSpec((PROJ_ROWS, D_MODEL), half),
            pl.BlockSpec((1, LANES), lambda i: (0, 0)),
            pl.BlockSpec((DN_WIDTH + SWA_WIDTH, D_MODEL), lambda i: (0, 0)),
        ],
        out_specs=pl.BlockSpec((PROJ_ROWS, D_MODEL), half),
        out_shape=jax.ShapeDtypeStruct((rows, D_MODEL), F32),
        compiler_params=pltpu.CompilerParams(
            dimension_semantics=("arbitrary",), vmem_limit_bytes=VMEM_LIMIT),
        name="out_proj",
    )(o_f, o_b, proj2d, sw, proj2d, x2d, out_norm_w, w_out)


def _rope_tables(seq):
    inv_freq = ROPE_THETA ** (-jnp.arange(0, SWA_HEAD_DIM, 2, dtype=F32) / SWA_HEAD_DIM)
    ang = jnp.arange(seq, dtype=F32)[:, None] * inv_freq[None, :]
    ang = jnp.concatenate([ang, ang], axis=-1)
    cos, sin = jnp.cos(ang), jnp.sin(ang)
    half = SWA_HEAD_DIM // 2
    sin_signed = jnp.concatenate([-sin[:, :half], sin[:, half:]], axis=-1)
    reps = LANES // SWA_HEAD_DIM
    return jnp.tile(cos, (1, reps)), jnp.tile(sin_signed, (1, reps))


def _layer(x, norm_w, w_in, conv_w, a_log, dt_bias, out_norm_w, q_norm_w, k_norm_w, sinks, w_out):
    b, s, _ = x.shape
    x2d = x.reshape(b * s, D_MODEL)
    o_q, o_k, o_v, o_z = 0, DN_WIDTH, 2 * DN_WIDTH, 3 * DN_WIDTH
    o_beta = 4 * DN_WIDTH
    o_decay = o_beta + N_GATE
    o_swq = o_decay + N_GATE
    o_swk = o_swq + SWA_WIDTH
    o_swv = o_swk + SWA_KV_WIDTH
    o_swz = o_swv + SWA_KV_WIDTH
    w_gate = w_in[:, o_beta:o_swq]
    w_perm = jnp.concatenate([
        w_in[:, o_q:o_beta],
        w_in[:, o_swq:o_swk], w_in[:, o_swz:o_swz + SWA_WIDTH],
        w_in[:, o_swk:o_swv], w_in[:, o_swv:o_swz]], axis=1).astype(BF16)
    w_gate_pad = jnp.concatenate(
        [w_gate, jnp.zeros((D_MODEL, LANES - 2 * N_GATE), w_in.dtype)], axis=1).astype(BF16)
    conv_pad = jnp.concatenate([conv_w, jnp.zeros((SUBLANES - DN_CONV, 3 * DN_WIDTH), conv_w.dtype)], axis=0)
    proj2d, gate_c, gate_t, q, k, v = _in_projection_conv(
        x2d, s, norm_w.reshape(1, D_MODEL), w_perm, w_gate_pad, w_gate.T.astype(BF16), conv_pad)
    proj3d = proj2d.reshape(b, s, PROJ_WIDTH - 3 * DN_WIDTH)
    q, k, v = (a.reshape(b, s, DN_WIDTH) for a in (q, k, v))

    a_flat = a_log.reshape(N_GATE).astype(F32)
    d_flat = dt_bias.reshape(N_GATE).astype(F32)
    zeros8 = jnp.zeros((N_GATE,), F32)
    pad_row = jnp.zeros((LANES - 2 * N_GATE,), F32)
    al_row = jnp.concatenate([zeros8, a_flat, pad_row]).reshape(1, LANES)
    dt_row = jnp.concatenate([zeros8, d_flat, pad_row]).reshape(1, LANES)
    al_col = jnp.concatenate([zeros8, a_flat]).reshape(2 * N_GATE, 1)
    dt_col = jnp.concatenate([zeros8, d_flat]).reshape(2 * N_GATE, 1)
    o_f, o_b = _dn_scan(q, k, v, gate_c.reshape(b, s, LANES), gate_t, al_row, dt_row, al_col, dt_col)

    cos_t, sin_t = _rope_tables(s)
    reps = LANES // SWA_HEAD_DIM
    sw = _swa(proj3d, cos_t, sin_t,
              jnp.tile(q_norm_w.astype(F32), reps).reshape(1, LANES),
              jnp.tile(k_norm_w.astype(F32), reps).reshape(1, LANES),
              sinks.astype(F32))

    y = _out_projection(o_f.reshape(b * s, DN_WIDTH), o_b.reshape(b * s, DN_WIDTH), proj2d,
                        sw.reshape(b * s, SWA_WIDTH), x2d,
                        out_norm_w.reshape(1, LANES).astype(F32), w_out.astype(BF16))
    return y.reshape(b, s, D_MODEL)


def kernel(x, norm_w, w_in, dn_conv_w, dn_a_log, dn_dt_bias, dn_out_norm_w,
           swa_q_norm_w, swa_k_norm_w, swa_sinks, w_out):
    for l in range(norm_w.shape[0]):
        x = _layer(x, norm_w[l], w_in[l], dn_conv_w[l], dn_a_log[l], dn_dt_bias[l], dn_out_norm_w[l],
                   swa_q_norm_w[l], swa_k_norm_w[l], swa_sinks[l], w_out[l])
    return x
```

```python
import functools

import jax
import jax.numpy as jnp
from jax import lax
from jax.experimental import pallas as pl
from jax.experimental.pallas import tpu as pltpu

D_MODEL = 1024
DN_HEADS = 4
DN_HEAD_DIM = 128
DN_WIDTH = DN_HEADS * DN_HEAD_DIM
DN_CONV = 5
N_DIR = 2
N_GATE = N_DIR * DN_HEADS
SWA_Q_HEADS = 8
SWA_KV_HEADS = 2
SWA_HEAD_DIM = 64
SWA_WIDTH = SWA_Q_HEADS * SWA_HEAD_DIM
SWA_KV_WIDTH = SWA_KV_HEADS * SWA_HEAD_DIM
WINDOW = 128
ROPE_THETA = 10000.0
EPS = 1e-6

LANES = 128
SUBLANES = 8
TILE = 128
PROJ_ROWS = 512
VMEM_LIMIT = 48 * 1024 * 1024

COL_DNZ = 0
COL_SWQ = COL_DNZ + DN_WIDTH
COL_SWZ = COL_SWQ + SWA_WIDTH
COL_SWK = COL_SWZ + SWA_WIDTH
COL_SWV = COL_SWK + SWA_KV_WIDTH
PROJ_WIDTH = 3 * DN_WIDTH + COL_SWV + SWA_KV_WIDTH
HALO = 16

F32 = jnp.float32
BF16 = jnp.bfloat16
NT_DIMS = (((1,), (1,)), ((), ()))
TN_DIMS = (((0,), (0,)), ((), ()))


def _silu(x):
    return x * jax.nn.sigmoid(x)


def _softplus(x):
    return jnp.maximum(x, 0.0) + jnp.log1p(jnp.exp(-jnp.abs(x)))


def _dot(a, b):
    return jnp.dot(a.astype(BF16), b.astype(BF16), preferred_element_type=F32)


def _dot_nt(a, b):
    return lax.dot_general(a.astype(BF16), b.astype(BF16), NT_DIMS, preferred_element_type=F32)


def _dot_tn(a, b):
    return lax.dot_general(a.astype(BF16), b.astype(BF16), TN_DIMS, preferred_element_type=F32)


def _proj_conv_kernel(tiles_per_seq, x_ref, nw_ref, w_ref, wg_ref, wgt_ref, cw_ref,
                      rest_ref, gc_ref, gt_ref, q_ref, k_ref, v_ref, buf_ref, new_ref, h_ref):
    i = pl.program_id(0)
    n_tiles = pl.num_programs(0) - 2
    rows = x_ref.shape[0]
    pad = (DN_CONV - 1) // 2
    qkv_w = 3 * DN_WIDTH

    @pl.when(i == 0)
    def _():
        buf_ref[...] = jnp.zeros_like(buf_ref)
        h_ref[...] = jnp.zeros_like(h_ref)

    h = h_ref[...]
    t = i - 1
    boundary = (t % tiles_per_seq == 0) | (t < 0) | (t >= n_tiles)
    head = jnp.dot(h[0:HALO, :], w_ref[:, 0:qkv_w], preferred_element_type=F32)
    buf_ref[HALO + rows:2 * HALO + rows, :] = jnp.where(boundary, 0.0, head)
    new_ref[...] = jnp.dot(h, w_ref[:, 0:qkv_w], preferred_element_type=F32)

    gc_ref[...] = jnp.dot(h, wg_ref[...], preferred_element_type=F32)
    gt_ref[...] = lax.dot_general(wgt_ref[...], h, NT_DIMS, preferred_element_type=F32)

    outs = (q_ref, k_ref, v_ref)
    rest_w = rest_ref.shape[1]
    chunk = 2 * LANES
    n_chunks = rest_w // chunk
    n_blocks = 3 * DN_HEADS
    for c in range(n_blocks):
        for r in range(c * n_chunks // n_blocks, (c + 1) * n_chunks // n_blocks):
            cols = slice(r * chunk, (r + 1) * chunk)
            rest_ref[:, cols] = jnp.dot(h, w_ref[:, qkv_w + r * chunk:qkv_w + (r + 1) * chunk],
                                        preferred_element_type=F32).astype(BF16)
        lo = c * LANES
        acc = None
        for j in range(DN_CONV):
            r0 = HALO - pad + j
            term = buf_ref[r0:r0 + rows, lo:lo + LANES] * cw_ref[j:j + 1, lo:lo + LANES]
            acc = term if acc is None else acc + term
        y = _silu(acc)
        part, head = divmod(c, DN_HEADS)
        if part < 2:
            y = y * lax.rsqrt(jnp.sum(y * y, axis=-1, keepdims=True) + EPS)
        if part == 0:
            y = y * (DN_HEAD_DIM ** -0.5)
        outs[part][:, head * LANES:(head + 1) * LANES] = y.astype(BF16)

    buf_ref[0:HALO, :] = jnp.where(boundary, 0.0, buf_ref[rows:rows + HALO, :])
    buf_ref[HALO:HALO + rows, :] = new_ref[...]

    x = x_ref[...]
    ms = jnp.mean(x * x, axis=-1, keepdims=True)
    h_ref[...] = ((x * lax.rsqrt(ms + EPS)) * nw_ref[...]).astype(BF16)


def _in_projection_conv(x2d, seq, norm_w, w_perm, w_gate, w_gate_t, conv_w):
    rows = x2d.shape[0]
    n_tiles = rows // PROJ_ROWS
    qkv_w = 3 * DN_WIDTH
    rest_w = PROJ_WIDTH - qkv_w
    clamp = lambda t: jnp.clip(t, 0, n_tiles - 1)
    cur = lambda i: (clamp(i), 0)
    prev = lambda i: (clamp(i - 1), 0)
    prev_t = lambda i: (0, clamp(i - 1))
    prev2 = lambda i: (clamp(i - 2), 0)
    const = lambda i: (0, 0)
    qkv_spec = pl.BlockSpec((PROJ_ROWS, DN_WIDTH), prev2)
    qkv_sds = jax.ShapeDtypeStruct((rows, DN_WIDTH), BF16)
    return pl.pallas_call(
        functools.partial(_proj_conv_kernel, seq // PROJ_ROWS),
        grid=(n_tiles + 2,),
        in_specs=[
            pl.BlockSpec((PROJ_ROWS, D_MODEL), cur),
            pl.BlockSpec((1, D_MODEL), const),
            pl.BlockSpec((D_MODEL, PROJ_WIDTH), const),
            pl.BlockSpec((D_MODEL, LANES), const),
            pl.BlockSpec((2 * N_GATE, D_MODEL), const),
            pl.BlockSpec((SUBLANES, qkv_w), const),
        ],
        out_specs=[
            pl.BlockSpec((PROJ_ROWS, rest_w), prev),
            pl.BlockSpec((PROJ_ROWS, LANES), prev),
            pl.BlockSpec((2 * N_GATE, PROJ_ROWS), prev_t),
            qkv_spec, qkv_spec, qkv_spec,
        ],
        out_shape=[
            jax.ShapeDtypeStruct((rows, rest_w), BF16),
            jax.ShapeDtypeStruct((rows, LANES), F32),
            jax.ShapeDtypeStruct((2 * N_GATE, rows), F32),
            qkv_sds, qkv_sds, qkv_sds,
        ],
        scratch_shapes=[pltpu.VMEM((PROJ_ROWS + 2 * HALO, qkv_w), F32),
                        pltpu.VMEM((PROJ_ROWS, qkv_w), F32),
                        pltpu.VMEM((PROJ_ROWS, D_MODEL), BF16)],
        compiler_params=pltpu.CompilerParams(
            dimension_semantics=("arbitrary",), vmem_limit_bytes=VMEM_LIMIT),
        name="in_proj_conv",
    )(x2d, norm_w, w_perm, w_gate, w_gate_t, conv_w)


def _split3(x):
    x1 = x.astype(BF16)
    r1 = x - x1.astype(F32)
    x2 = r1.astype(BF16)
    x3 = (r1 - x2.astype(F32)).astype(BF16)
    return x1, x2, x3


def _exact_left(m_bf16, x):
    return sum(jnp.dot(m_bf16, p, preferred_element_type=F32) for p in _split3(x))


def _exact_right(x, m_bf16):
    return sum(jnp.dot(p, m_bf16, preferred_element_type=F32) for p in _split3(x))


def _tri_inverse_minus_eye(x0, row, col, lower):
    same = lambda s: (row // s) == (col // s)
    s = 2
    p = jnp.where(same(s), x0, 0.0)
    while s < TILE:
        m_off = jnp.where(same(2 * s) & jnp.logical_not(same(s)), -x0, 0.0)
        if s < SUBLANES:
            z = m_off + _dot(m_off, p)
            yield
            p = p - z - _dot(p, z)
            yield
        else:
            starts = range(s if lower else 0, TILE, 2 * s)
            take = lambda a: jnp.concatenate([a[r:r + s] for r in starts], axis=0)

            def put(rows, base):
                parts = [rows[(r // (2 * s)) * s:(r // (2 * s) + 1) * s] if r in starts else base[r:r + s]
                         for r in range(0, TILE, s)]
                return jnp.concatenate(parts, axis=0)

            m_rows = take(m_off)
            z_rows = m_rows + _dot(m_rows, p)
            yield
            new_rows = take(p) - z_rows - _dot(take(p), put(z_rows, jnp.zeros_like(p)))
            yield
            p = put(new_rows, p)
        s *= 2
    return p


def _dn_chain(q, k, v, beta, gcc, gcr, tot, incl, lower, diag, row, col, state_ref, o_ref, hs):
    decay = jnp.where(incl, jnp.exp(jnp.where(incl, gcc - gcr, 0.0)), 0.0)
    gram = _dot_nt(k, k)
    qk = _dot_nt(q, k)
    yield
    x0 = jnp.where(diag, 0.0, -(gram * decay * beta))
    p = yield from _tri_inverse_minus_eye(x0, row, col, lower)
    egc = jnp.exp(gcc)
    rhs = jnp.concatenate([v * beta, k * (beta * egc)], axis=1)
    uw = rhs + _dot(p, rhs)
    yield
    auw = _dot(qk * decay, uw)
    kt = _dot_tn(k * jnp.exp(tot - gcc), uw)
    yield
    p_q = q * egc - auw[:, TILE:]
    state = state_ref[...]
    res = _dot(jnp.concatenate([kt[:, TILE:], p_q], axis=0), state)
    yield
    o_ref[:, hs] = (res[TILE:] + auw[:, :TILE]).astype(BF16)
    state_ref[...] = jnp.exp(tot[0:1, :]) * state - res[:TILE] + kt[:, :TILE]


def _dn_kernel(batch, *refs):
    n_in = 4 + batch
    fwd_refs, bwd_refs = refs[0:n_in], refs[n_in:2 * n_in]
    alr_ref, dtr_ref, alc_ref, dtc_ref, of_ref, ob_ref, s_ref = refs[2 * n_in:]

    @pl.when(pl.program_id(0) == 0)
    def _():
        s_ref[...] = jnp.zeros_like(s_ref)

    row = lax.broadcasted_iota(jnp.int32, (TILE, TILE), 0)
    col = lax.broadcasted_iota(jnp.int32, (TILE, TILE), 1)
    lower = (col <= row)
    upper = (col >= row)
    diag = (col == row)
    lower_bf = lower.astype(BF16)
    upper_bf = upper.astype(BF16)
    ones_bf = jnp.ones((TILE, TILE), BF16)
    neg_a_row = -jnp.exp(alr_ref[...])
    neg_a_col = -jnp.exp(alc_ref[...])

    streams = ((fwd_refs, of_ref, lower, lower_bf, upper_bf),
               (bwd_refs, ob_ref, upper, upper_bf, lower_bf))
    chains = []
    for d, (in_refs, o_ref, incl, cum_left, cum_right) in enumerate(streams):
        q_ref, k_ref, v_ref, gc_ref = in_refs[0:4]
        for b in range(batch):
            gate_c = gc_ref[b]
            gate_r = in_refs[4 + b][...]
            beta_c = jax.nn.sigmoid(gate_c)
            g_c = neg_a_row * _softplus(gate_c + dtr_ref[...])
            g_r = neg_a_col * _softplus(gate_r + dtc_ref[...])
            cum_c = _exact_left(cum_left, g_c)
            tot_c = _exact_left(ones_bf, g_c)
            cum_r = _exact_right(g_r, cum_right)
            for h in range(DN_HEADS):
                cb = d * DN_HEADS + h
                cg = N_GATE + cb
                hs = slice(h * LANES, (h + 1) * LANES)
                chains.append(_dn_chain(
                    q_ref[b, :, hs].astype(F32), k_ref[b, :, hs].astype(F32), v_ref[b, :, hs].astype(F32),
                    beta_c[:, cb:cb + 1], cum_c[:, cg:cg + 1], cum_r[cg:cg + 1, :], tot_c[:, cg:cg + 1],
                    incl, d == 0, diag, row, col, s_ref.at[b * N_GATE + cb], o_ref.at[b], hs))
    while chains:
        alive = []
        for chain in chains:
            try:
                next(chain)
                alive.append(chain)
            except StopIteration:
                pass
        chains = alive


def _dn_scan(q, k, v, gate_c, gate_t, al_row, dt_row, al_col, dt_col):
    b, s, _ = q.shape
    n_tiles = s // TILE
    fwd = lambda i: (0, i, 0)
    bwd = lambda i: (0, n_tiles - 1 - i, 0)
    const = lambda i: (0, 0)
    qkv_f = pl.BlockSpec((b, TILE, DN_WIDTH), fwd)
    qkv_b = pl.BlockSpec((b, TILE, DN_WIDTH), bwd)
    gc_f = pl.BlockSpec((b, TILE, LANES), fwd)
    gc_b = pl.BlockSpec((b, TILE, LANES), bwd)
    gr_f = [pl.BlockSpec((2 * N_GATE, TILE), functools.partial(lambda bb, i: (0, bb * n_tiles + i), bb))
            for bb in range(b)]
    gr_b = [pl.BlockSpec((2 * N_GATE, TILE),
                         functools.partial(lambda bb, i: (0, bb * n_tiles + n_tiles - 1 - i), bb))
            for bb in range(b)]
    row_p = pl.BlockSpec((1, LANES), const)
    col_p = pl.BlockSpec((2 * N_GATE, 1), const)
    out_sds = jax.ShapeDtypeStruct((b, s, DN_WIDTH), BF16)
    return pl.pallas_call(
        functools.partial(_dn_kernel, b),
        grid=(n_tiles,),
        in_specs=[qkv_f, qkv_f, qkv_f, gc_f, *gr_f,
                  qkv_b, qkv_b, qkv_b, gc_b, *gr_b,
                  row_p, row_p, col_p, col_p],
        out_specs=[qkv_f, qkv_b],
        out_shape=[out_sds, out_sds],
        scratch_shapes=[pltpu.VMEM((b * N_GATE, DN_HEAD_DIM, DN_HEAD_DIM), F32)],
        compiler_params=pltpu.CompilerParams(
            dimension_semantics=("arbitrary",), vmem_limit_bytes=VMEM_LIMIT),
        name="dn_scan",
    )(q, k, v, gate_c, *([gate_t] * b), q, k, v, gate_c, *([gate_t] * b), al_row, dt_row, al_col, dt_col)


def _pair_rms(x, w, head_ones):
    x2 = x * x
    hi = x2.astype(BF16)
    lo = (x2 - hi.astype(F32)).astype(BF16)
    ss = (jnp.dot(hi, head_ones, preferred_element_type=F32)
          + jnp.dot(lo, head_ones, preferred_element_type=F32))
    return (x * lax.rsqrt(ss * (1.0 / SWA_HEAD_DIM) + EPS)) * w


def _pair_rope(x, cos, sin_signed, first_quarter):
    half = SWA_HEAD_DIM // 2
    rot = jnp.where(first_quarter, pltpu.roll(x, LANES - half, 1), pltpu.roll(x, half, 1))
    return x * cos + rot * sin_signed


def _swa_head(q_m, k_span, v_span, valid, sink, o_ref, lanes, needs_roll):
    sc = lax.dot_general(q_m, k_span, NT_DIMS, preferred_element_type=F32)
    yield
    sc = jnp.where(valid, sc, -1e30)
    mx = jnp.maximum(jnp.max(sc, axis=-1, keepdims=True), sink)
    yield
    p = jnp.exp(sc - mx)
    denom = jnp.sum(p, axis=-1, keepdims=True) + jnp.exp(sink - mx)
    yield
    pv = jnp.dot(p.astype(BF16), v_span, preferred_element_type=F32) / denom
    if needs_roll:
        pv = pltpu.roll(pv, SWA_HEAD_DIM, 1)
    o_ref[:, lanes] = pv[:, lanes.start % LANES:(lanes.start % LANES) + SWA_HEAD_DIM].astype(BF16)


def _swa_kernel(q_ref, kc_ref, kn_ref, vp_ref, vc_ref, vn_ref,
                cc_ref, sc_ref, cn_ref, sn_ref,
                qw_ref, kw_ref, sink_ref, o_ref, ks_ref):
    i = pl.program_id(0)
    n = pl.num_programs(0)
    batch = q_ref.shape[0]
    lane = lax.broadcasted_iota(jnp.int32, (TILE, LANES), 1)
    lo_half = lane < SWA_HEAD_DIM
    first_quarter = (lane % SWA_HEAD_DIM) < (SWA_HEAD_DIM // 2)

    sub = lax.broadcasted_iota(jnp.int32, (LANES, LANES), 0)
    head_ones = ((sub // SWA_HEAD_DIM) == (lane // SWA_HEAD_DIM)).astype(BF16)

    def prep(x, w, cos_ref, sin_ref):
        return _pair_rope(_pair_rms(x, w, head_ones), cos_ref[...], sin_ref[...], first_quarter)

    kw = kw_ref[...]

    @pl.when(i == 0)
    def _():
        for b in range(batch):
            ks_ref[3 * b] = jnp.zeros((TILE, LANES), BF16)
            ks_ref[3 * b + 1] = prep(kc_ref[b].astype(F32), kw, cc_ref, sc_ref).astype(BF16)

    @pl.when(i > 0)
    def _():
        for b in range(batch):
            ks_ref[3 * b] = ks_ref[3 * b + 1]
            ks_ref[3 * b + 1] = ks_ref[3 * b + 2]

    for b in range(batch):
        ks_ref[3 * b + 2] = prep(kn_ref[b].astype(F32), kw, cn_ref, sn_ref).astype(BF16)
    span = 3 * TILE
    r = lax.broadcasted_iota(jnp.int32, (TILE, span), 0)
    c = lax.broadcasted_iota(jnp.int32, (TILE, span), 1)
    valid = (jnp.abs(c - WINDOW - r) <= WINDOW)
    valid = valid & ((c >= TILE) | (i > 0)) & ((c < 2 * TILE) | (i < n - 1))
    group = SWA_Q_HEADS // SWA_KV_HEADS
    qw = qw_ref[...]
    scale = SWA_HEAD_DIM ** -0.5
    heads = []
    for b in range(batch):
        k_span = ks_ref[3 * b:3 * b + 3].reshape(3 * TILE, LANES)
        v_span = jnp.concatenate([vp_ref[b], vc_ref[b], vn_ref[b]], axis=0)
        for j in range(SWA_Q_HEADS // 2):
            q_pair = prep(q_ref[b, :, j * LANES:(j + 1) * LANES].astype(F32), qw, cc_ref, sc_ref) * scale
            q_swap = pltpu.roll(q_pair, SWA_HEAD_DIM, 1)
            for half in range(2):
                hq = 2 * j + half
                g = hq // group
                q_al = q_pair if half == g else q_swap
                q_m = jnp.where(lo_half if g == 0 else jnp.logical_not(lo_half), q_al, 0.0).astype(BF16)
                lanes = slice(hq * SWA_HEAD_DIM, (hq + 1) * SWA_HEAD_DIM)
                heads.append(_swa_head(q_m, k_span, v_span, valid, sink_ref[hq], o_ref.at[b], lanes,
                                       half != g))
    while heads:
        alive = []
        for head in heads:
            try:
                next(head)
                alive.append(head)
            except StopIteration:
                pass
        heads = alive


def _swa(proj3d, cos_t, sin_t, qw, kw, sinks):
    b, s, _ = proj3d.shape
    n_tiles = s // TILE
    kblk = COL_SWK // LANES
    vblk = COL_SWV // LANES
    prev = lambda i: jnp.maximum(i - 1, 0)
    nxt = lambda i: jnp.minimum(i + 1, n_tiles - 1)

    def kv_spec(blk, f):
        return pl.BlockSpec((b, TILE, LANES), lambda i: (0, f(i), blk))

    def tab_spec(f):
        return pl.BlockSpec((TILE, LANES), lambda i: (f(i), 0))

    ident = lambda i: i
    row_p = pl.BlockSpec((1, LANES), lambda i: (0, 0))
    return pl.pallas_call(
        _swa_kernel,
        grid=(n_tiles,),
        in_specs=[
            pl.BlockSpec((b, TILE, SWA_WIDTH), lambda i: (0, i, COL_SWQ // SWA_WIDTH)),
            kv_spec(kblk, ident), kv_spec(kblk, nxt),
            kv_spec(vblk, prev), kv_spec(vblk, ident), kv_spec(vblk, nxt),
            tab_spec(ident), tab_spec(ident), tab_spec(nxt), tab_spec(nxt),
            row_p, row_p,
            pl.BlockSpec(memory_space=pltpu.SMEM),
        ],
        out_specs=pl.BlockSpec((b, TILE, SWA_WIDTH), lambda i: (0, i, 0)),
        out_shape=jax.ShapeDtypeStruct((b, s, SWA_WIDTH), BF16),
        scratch_shapes=[pltpu.VMEM((3 * b, TILE, LANES), BF16)],
        compiler_params=pltpu.CompilerParams(
            dimension_semantics=("arbitrary",), vmem_limit_bytes=VMEM_LIMIT),
        name="swa",
    )(proj3d, proj3d, proj3d, proj3d, proj3d, proj3d,
      cos_t, sin_t, cos_t, sin_t, qw, kw, sinks)


def _out_kernel(of_ref, ob_ref, z_ref, sw_ref, swz_ref, x_ref, onw_ref, wo_ref, y_ref):
    dn = of_ref[...].astype(F32) + ob_ref[...].astype(F32)
    onw = onw_ref[...]
    parts = []
    for h in range(DN_HEADS):
        blk = dn[:, h * LANES:(h + 1) * LANES]
        ms = jnp.mean(blk * blk, axis=-1, keepdims=True)
        parts.append((blk * lax.rsqrt(ms + EPS)) * onw)
    dn_n = jnp.concatenate(parts, axis=1) * _silu(z_ref[...].astype(F32))
    sw = sw_ref[...].astype(F32) * _silu(swz_ref[...].astype(F32))
    mix = jnp.concatenate([dn_n, sw], axis=1).astype(BF16)
    y_ref[...] = x_ref[...] + jnp.dot(mix, wo_ref[...], preferred_element_type=F32)


def _out_projection(o_f, o_b, proj2d, sw, x2d, out_norm_w, w_out):
    rows = x2d.shape[0]
    half = lambda i: (i, 0)
    return pl.pallas_call(
        _out_kernel,
        grid=(rows // PROJ_ROWS,),
        in_specs=[
            pl.BlockSpec((PROJ_ROWS, DN_WIDTH), half),
            pl.BlockSpec((PROJ_ROWS, DN_WIDTH), half),
            pl.BlockSpec((PROJ_ROWS, DN_WIDTH), lambda i: (i, COL_DNZ // DN_WIDTH)),
            pl.BlockSpec((PROJ_ROWS, SWA_WIDTH), half),
            pl.BlockSpec((PROJ_ROWS, SWA_WIDTH), lambda i: (i, COL_SWZ // SWA_WIDTH)),
            pl.BlockSpec((PROJ_ROWS, D_MODEL), half),
            pl.BlockSpec((1, LANES), lambda i: (0, 0)),
            pl.BlockSpec((DN_WIDTH + SWA_WIDTH, D_MODEL), lambda i: (0, 0)),
        ],
        out_specs=pl.BlockSpec((PROJ_ROWS, D_MODEL), half),
        out_shape=jax.ShapeDtypeStruct((rows, D_MODEL), F32),
        compiler_params=pltpu.CompilerParams(
            dimension_semantics=("arbitrary",), vmem_limit_bytes=VMEM_LIMIT),
        name="out_proj",
    )(o_f, o_b, proj2d, sw, proj2d, x2d, out_norm_w, w_out)


def _rope_tables(seq):
    inv_freq = ROPE_THETA ** (-jnp.arange(0, SWA_HEAD_DIM, 2, dtype=F32) / SWA_HEAD_DIM)
    ang = jnp.arange(seq, dtype=F32)[:, None] * inv_freq[None, :]
    ang = jnp.concatenate([ang, ang], axis=-1)
    cos, sin = jnp.cos(ang), jnp.sin(ang)
    half = SWA_HEAD_DIM // 2
    sin_signed = jnp.concatenate([-sin[:, :half], sin[:, half:]], axis=-1)
    reps = LANES // SWA_HEAD_DIM
    return jnp.tile(cos, (1, reps)), jnp.tile(sin_signed, (1, reps))


def _layer(x, norm_w, w_in, conv_w, a_log, dt_bias, out_norm_w, q_norm_w, k_norm_w, sinks, w_out):
    b, s, _ = x.shape
    x2d = x.reshape(b * s, D_MODEL)
    o_q, o_k, o_v, o_z = 0, DN_WIDTH, 2 * DN_WIDTH, 3 * DN_WIDTH
    o_beta = 4 * DN_WIDTH
    o_decay = o_beta + N_GATE
    o_swq = o_decay + N_GATE
    o_swk = o_swq + SWA_WIDTH
    o_swv = o_swk + SWA_KV_WIDTH
    o_swz = o_swv + SWA_KV_WIDTH
    w_gate = w_in[:, o_beta:o_swq]
    w_perm = jnp.concatenate([
        w_in[:, o_q:o_beta],
        w_in[:, o_swq:o_swk], w_in[:, o_swz:o_swz + SWA_WIDTH],
        w_in[:, o_swk:o_swv], w_in[:, o_swv:o_swz]], axis=1).astype(BF16)
    w_gate_pad = jnp.concatenate(
        [w_gate, jnp.zeros((D_MODEL, LANES - 2 * N_GATE), w_in.dtype)], axis=1).astype(BF16)
    conv_pad = jnp.concatenate([conv_w, jnp.zeros((SUBLANES - DN_CONV, 3 * DN_WIDTH), conv_w.dtype)], axis=0)
    proj2d, gate_c, gate_t, q, k, v = _in_projection_conv(
        x2d, s, norm_w.reshape(1, D_MODEL), w_perm, w_gate_pad, w_gate.T.astype(BF16), conv_pad)
    proj3d = proj2d.reshape(b, s, PROJ_WIDTH - 3 * DN_WIDTH)
    q, k, v = (a.reshape(b, s, DN_WIDTH) for a in (q, k, v))

    a_flat = a_log.reshape(N_GATE).astype(F32)
    d_flat = dt_bias.reshape(N_GATE).astype(F32)
    zeros8 = jnp.zeros((N_GATE,), F32)
    pad_row = jnp.zeros((LANES - 2 * N_GATE,), F32)
    al_row = jnp.concatenate([zeros8, a_flat, pad_row]).reshape(1, LANES)
    dt_row = jnp.concatenate([zeros8, d_flat, pad_row]).reshape(1, LANES)
    al_col = jnp.concatenate([zeros8, a_flat]).reshape(2 * N_GATE, 1)
    dt_col = jnp.concatenate([zeros8, d_flat]).reshape(2 * N_GATE, 1)
    o_f, o_b = _dn_scan(q, k, v, gate_c.reshape(b, s, LANES), gate_t, al_row, dt_row, al_col, dt_col)

    cos_t, sin_t = _rope_tables(s)
    reps = LANES // SWA_HEAD_DIM
    sw = _swa(proj3d, cos_t, sin_t,
              jnp.tile(q_norm_w.astype(F32), reps).reshape(1, LANES),
              jnp.tile(k_norm_w.astype(F32), reps).reshape(1, LANES),
              sinks.astype(F32))

    y = _out_projection(o_f.reshape(b * s, DN_WIDTH), o_b.reshape(b * s, DN_WIDTH), proj2d,
                        sw.reshape(b * s, SWA_WIDTH), x2d,
                        out_norm_w.reshape(1, LANES).astype(F32), w_out.astype(BF16))
    return y.reshape(b, s, D_MODEL)


def kernel(x, norm_w, w_in, dn_conv_w, dn_a_log, dn_dt_bias, dn_out_norm_w,
           swa_q_norm_w, swa_k_norm_w, swa_sinks, w_out):
    for l in range(norm_w.shape[0]):
        x = _layer(x, norm_w[l], w_in[l], dn_conv_w[l], dn_a_log[l], dn_dt_bias[l], dn_out_norm_w[l],
                   swa_q_norm_w[l], swa_k_norm_w[l], swa_sinks[l], w_out[l])
    return x
```

```python
import functools

import jax
import jax.numpy as jnp
from jax import lax
from jax.experimental import pallas as pl
from jax.experimental.pallas import tpu as pltpu

D_MODEL = 1024
DN_HEADS = 4
DN_HEAD_DIM = 128
DN_WIDTH = DN_HEADS * DN_HEAD_DIM
DN_CONV = 5
N_DIR = 2
N_GATE = N_DIR * DN_HEADS
SWA_Q_HEADS = 8
SWA_KV_HEADS = 2
SWA_HEAD_DIM = 64
SWA_WIDTH = SWA_Q_HEADS * SWA_HEAD_DIM
SWA_KV_WIDTH = SWA_KV_HEADS * SWA_HEAD_DIM
WINDOW = 128
ROPE_THETA = 10000.0
EPS = 1e-6

LANES = 128
SUBLANES = 8
TILE = 128
PROJ_ROWS = 512
VMEM_LIMIT = 48 * 1024 * 1024

COL_DNZ = 0
COL_SWQ = COL_DNZ + DN_WIDTH
COL_SWZ = COL_SWQ + SWA_WIDTH
COL_SWK = COL_SWZ + SWA_WIDTH
COL_SWV = COL_SWK + SWA_KV_WIDTH
PROJ_WIDTH = 3 * DN_WIDTH + COL_SWV + SWA_KV_WIDTH
HALO = 16

F32 = jnp.float32
BF16 = jnp.bfloat16
NT_DIMS = (((1,), (1,)), ((), ()))
TN_DIMS = (((0,), (0,)), ((), ()))


def _silu(x):
    return x * jax.nn.sigmoid(x)


def _softplus(x):
    return jnp.maximum(x, 0.0) + jnp.log1p(jnp.exp(-jnp.abs(x)))


def _dot(a, b):
    return jnp.dot(a.astype(BF16), b.astype(BF16), preferred_element_type=F32)


def _dot_nt(a, b):
    return lax.dot_general(a.astype(BF16), b.astype(BF16), NT_DIMS, preferred_element_type=F32)


def _dot_tn(a, b):
    return lax.dot_general(a.astype(BF16), b.astype(BF16), TN_DIMS, preferred_element_type=F32)


def _proj_conv_kernel(tiles_per_seq, x_ref, nw_ref, w_ref, wg_ref, wgt_ref, cw_ref,
                      rest_ref, gc_ref, gt_ref, q_ref, k_ref, v_ref, buf_ref, new_ref, h_ref):
    i = pl.program_id(0)
    n_tiles = pl.num_programs(0) - 2
    rows = x_ref.shape[0]
    pad = (DN_CONV - 1) // 2
    qkv_w = 3 * DN_WIDTH

    @pl.when(i == 0)
    def _():
        buf_ref[...] = jnp.zeros_like(buf_ref)
        h_ref[...] = jnp.zeros_like(h_ref)

    h = h_ref[...]
    t = i - 1
    boundary = (t % tiles_per_seq == 0) | (t < 0) | (t >= n_tiles)
    head = jnp.dot(h[0:HALO, :], w_ref[:, 0:qkv_w], preferred_element_type=F32)
    buf_ref[HALO + rows:2 * HALO + rows, :] = jnp.where(boundary, 0.0, head)
    new_ref[...] = jnp.dot(h, w_ref[:, 0:qkv_w], preferred_element_type=F32)

    gc_ref[...] = jnp.dot(h, wg_ref[...], preferred_element_type=F32)
    gt_ref[...] = lax.dot_general(wgt_ref[...], h, NT_DIMS, preferred_element_type=F32)

    outs = (q_ref, k_ref, v_ref)
    rest_w = rest_ref.shape[1]
    chunk = 2 * LANES
    n_chunks = rest_w // chunk
    n_blocks = 3 * DN_HEADS
    for c in range(n_blocks):
        for r in range(c * n_chunks // n_blocks, (c + 1) * n_chunks // n_blocks):
            cols = slice(r * chunk, (r + 1) * chunk)
            rest_ref[:, cols] = jnp.dot(h, w_ref[:, qkv_w + r * chunk:qkv_w + (r + 1) * chunk],
                                        preferred_element_type=F32).astype(BF16)
        lo = c * LANES
        acc = None
        for j in range(DN_CONV):
            r0 = HALO - pad + j
            term = buf_ref[r0:r0 + rows, lo:lo + LANES] * cw_ref[j:j + 1, lo:lo + LANES]
            acc = term if acc is None else acc + term
        y = _silu(acc)
        part, head = divmod(c, DN_HEADS)
        if part < 2:
            y = y * lax.rsqrt(jnp.sum(y * y, axis=-1, keepdims=True) + EPS)
        if part == 0:
            y = y * (DN_HEAD_DIM ** -0.5)
        outs[part][:, head * LANES:(head + 1) * LANES] = y.astype(BF16)

    buf_ref[0:HALO, :] = jnp.where(boundary, 0.0, buf_ref[rows:rows + HALO, :])
    buf_ref[HALO:HALO + rows, :] = new_ref[...]

    x = x_ref[...]
    ms = jnp.mean(x * x, axis=-1, keepdims=True)
    h_ref[...] = ((x * lax.rsqrt(ms + EPS)) * nw_ref[...]).astype(BF16)


def _in_projection_conv(x2d, seq, norm_w, w_perm, w_gate, w_gate_t, conv_w):
    rows = x2d.shape[0]
    n_tiles = rows // PROJ_ROWS
    qkv_w = 3 * DN_WIDTH
    rest_w = PROJ_WIDTH - qkv_w
    clamp = lambda t: jnp.clip(t, 0, n_tiles - 1)
    cur = lambda i: (clamp(i), 0)
    prev = lambda i: (clamp(i - 1), 0)
    prev_t = lambda i: (0, clamp(i - 1))
    prev2 = lambda i: (clamp(i - 2), 0)
    const = lambda i: (0, 0)
    qkv_spec = pl.BlockSpec((PROJ_ROWS, DN_WIDTH), prev2)
    qkv_sds = jax.ShapeDtypeStruct((rows, DN_WIDTH), BF16)
    return pl.pallas_call(
        functools.partial(_proj_conv_kernel, seq // PROJ_ROWS),
        grid=(n_tiles + 2,),
        in_specs=[
            pl.BlockSpec((PROJ_ROWS, D_MODEL), cur),
            pl.BlockSpec((1, D_MODEL), const),
            pl.BlockSpec((D_MODEL, PROJ_WIDTH), const),
            pl.BlockSpec((D_MODEL, LANES), const),
            pl.BlockSpec((2 * N_GATE, D_MODEL), const),
            pl.BlockSpec((SUBLANES, qkv_w), const),
        ],
        out_specs=[
            pl.BlockSpec((PROJ_ROWS, rest_w), prev),
            pl.BlockSpec((PROJ_ROWS, LANES), prev),
            pl.BlockSpec((2 * N_GATE, PROJ_ROWS), prev_t),
            qkv_spec, qkv_spec, qkv_spec,
        ],
        out_shape=[
            jax.ShapeDtypeStruct((rows, rest_w), BF16),
            jax.ShapeDtypeStruct((rows, LANES), F32),
            jax.ShapeDtypeStruct((2 * N_GATE, rows), F32),
            qkv_sds, qkv_sds, qkv_sds,
        ],
        scratch_shapes=[pltpu.VMEM((PROJ_ROWS + 2 * HALO, qkv_w), F32),
                        pltpu.VMEM((PROJ_ROWS, qkv_w), F32),
                        pltpu.VMEM((PROJ_ROWS, D_MODEL), BF16)],
        compiler_params=pltpu.CompilerParams(
            dimension_semantics=("arbitrary",), vmem_limit_bytes=VMEM_LIMIT),
        name="in_proj_conv",
    )(x2d, norm_w, w_perm, w_gate, w_gate_t, conv_w)


def _split3(x):
    x1 = x.astype(BF16)
    r1 = x - x1.astype(F32)
    x2 = r1.astype(BF16)
    x3 = (r1 - x2.astype(F32)).astype(BF16)
    return x1, x2, x3


def _exact_left(m_bf16, x):
    return sum(jnp.dot(m_bf16, p, preferred_element_type=F32) for p in _split3(x))


def _exact_right(x, m_bf16):
    return sum(jnp.dot(p, m_bf16, preferred_element_type=F32) for p in _split3(x))


def _tri_inverse_minus_eye(x0, row, col, lower):
    same = lambda s: (row // s) == (col // s)
    s = 2
    p = jnp.where(same(s), x0, 0.0)
    while s < TILE:
        m_off = jnp.where(same(2 * s) & jnp.logical_not(same(s)), -x0, 0.0)
        if s < SUBLANES:
            z = m_off + _dot(m_off, p)
            yield
            p = p - z - _dot(p, z)
            yield
        else:
            starts = range(s if lower else 0, TILE, 2 * s)
            take = lambda a: jnp.concatenate([a[r:r + s] for r in starts], axis=0)

            def put(rows, base):
                parts = [rows[(r // (2 * s)) * s:(r // (2 * s) + 1) * s] if r in starts else base[r:r + s]
                         for r in range(0, TILE, s)]
                return jnp.concatenate(parts, axis=0)

            m_rows = take(m_off)
            z_rows = m_rows + _dot(m_rows, p)
            yield
            new_rows = take(p) - z_rows - _dot(take(p), put(z_rows, jnp.zeros_like(p)))
            yield
            p = put(new_rows, p)
        s *= 2
    return p


def _dn_chain(q, k, v, beta, gcc, gcr, tot, incl, lower, diag, row, col, state_ref, o_ref, hs):
    decay = jnp.where(incl, jnp.exp(jnp.where(incl, gcc - gcr, 0.0)), 0.0)
    gram = _dot_nt(k, k)
    qk = _dot_nt(q, k)
    yield
    x0 = jnp.where(diag, 0.0, -(gram * decay * beta))
    p = yield from _tri_inverse_minus_eye(x0, row, col, lower)
    egc = jnp.exp(gcc)
    rhs = jnp.concatenate([v * beta, k * (beta * egc)], axis=1)
    uw = rhs + _dot(p, rhs)
    yield
    auw = _dot(qk * decay, uw)
    kt = _dot_tn(k * jnp.exp(tot - gcc), uw)
    yield
    p_q = q * egc - auw[:, TILE:]
    state = state_ref[...]
    res = _dot(jnp.concatenate([kt[:, TILE:], p_q], axis=0), state)
    yield
    o_ref[:, hs] = (res[TILE:] + auw[:, :TILE]).astype(BF16)
    state_ref[...] = jnp.exp(tot) * state - res[:TILE] + kt[:, :TILE]


def _dn_kernel(batch, *refs):
    n_in = 4 + batch
    fwd_refs, bwd_refs = refs[0:n_in], refs[n_in:2 * n_in]
    alr_ref, dtr_ref, alc_ref, dtc_ref, of_ref, ob_ref, s_ref = refs[2 * n_in:]

    @pl.when(pl.program_id(0) == 0)
    def _():
        s_ref[...] = jnp.zeros_like(s_ref)

    row = lax.broadcasted_iota(jnp.int32, (TILE, TILE), 0)
    col = lax.broadcasted_iota(jnp.int32, (TILE, TILE), 1)
    lower = (col <= row)
    upper = (col >= row)
    diag = (col == row)
    lower_bf = lower.astype(BF16)
    upper_bf = upper.astype(BF16)
    neg_a_row = -jnp.exp(alr_ref[...])
    neg_a_col = -jnp.exp(alc_ref[...])

    streams = ((fwd_refs, of_ref, lower, lower_bf, upper_bf, TILE - 1),
               (bwd_refs, ob_ref, upper, upper_bf, lower_bf, 0))
    chains = []
    for d, (in_refs, o_ref, incl, cum_left, cum_right, last) in enumerate(streams):
        q_ref, k_ref, v_ref, gc_ref = in_refs[0:4]
        for b in range(batch):
            gate_c = gc_ref[b]
            gate_r = in_refs[4 + b][...]
            beta_c = jax.nn.sigmoid(gate_c)
            g_c = neg_a_row * _softplus(gate_c + dtr_ref[...])
            g_r = neg_a_col * _softplus(gate_r + dtc_ref[...])
            cum_c = _exact_left(cum_left, g_c)
            tot_c = cum_c[last:last + 1, :]
            cum_r = _exact_right(g_r, cum_right)
            for h in range(DN_HEADS):
                cb = d * DN_HEADS + h
                cg = N_GATE + cb
                hs = slice(h * LANES, (h + 1) * LANES)
                chains.append(_dn_chain(
                    q_ref[b, :, hs].astype(F32), k_ref[b, :, hs].astype(F32), v_ref[b, :, hs].astype(F32),
                    beta_c[:, cb:cb + 1], cum_c[:, cg:cg + 1], cum_r[cg:cg + 1, :], tot_c[:, cg:cg + 1],
                    incl, d == 0, diag, row, col, s_ref.at[b * N_GATE + cb], o_ref.at[b], hs))
    while chains:
        alive = []
        for chain in chains:
            try:
                next(chain)
                alive.append(chain)
            except StopIteration:
                pass
        chains = alive


def _dn_scan(q, k, v, gate_c, gate_t, al_row, dt_row, al_col, dt_col):
    b, s, _ = q.shape
    n_tiles = s // TILE
    fwd = lambda i: (0, i, 0)
    bwd = lambda i: (0, n_tiles - 1 - i, 0)
    const = lambda i: (0, 0)
    qkv_f = pl.BlockSpec((b, TILE, DN_WIDTH), fwd)
    qkv_b = pl.BlockSpec((b, TILE, DN_WIDTH), bwd)
    gc_f = pl.BlockSpec((b, TILE, LANES), fwd)
    gc_b = pl.BlockSpec((b, TILE, LANES), bwd)
    gr_f = [pl.BlockSpec((2 * N_GATE, TILE), functools.partial(lambda bb, i: (0, bb * n_tiles + i), bb))
            for bb in range(b)]
    gr_b = [pl.BlockSpec((2 * N_GATE, TILE),
                         functools.partial(lambda bb, i: (0, bb * n_tiles + n_tiles - 1 - i), bb))
            for bb in range(b)]
    row_p = pl.BlockSpec((1, LANES), const)
    col_p = pl.BlockSpec((2 * N_GATE, 1), const)
    out_sds = jax.ShapeDtypeStruct((b, s, DN_WIDTH), BF16)
    return pl.pallas_call(
        functools.partial(_dn_kernel, b),
        grid=(n_tiles,),
        in_specs=[qkv_f, qkv_f, qkv_f, gc_f, *gr_f,
                  qkv_b, qkv_b, qkv_b, gc_b, *gr_b,
                  row_p, row_p, col_p, col_p],
        out_specs=[qkv_f, qkv_b],
        out_shape=[out_sds, out_sds],
        scratch_shapes=[pltpu.VMEM((b * N_GATE, DN_HEAD_DIM, DN_HEAD_DIM), F32)],
        compiler_params=pltpu.CompilerParams(
            dimension_semantics=("arbitrary",), vmem_limit_bytes=VMEM_LIMIT),
        name="dn_scan",
    )(q, k, v, gate_c, *([gate_t] * b), q, k, v, gate_c, *([gate_t] * b), al_row, dt_row, al_col, dt_col)


def _pair_rms(x, w, head_ones):
    x2 = x * x
    hi = x2.astype(BF16)
    lo = (x2 - hi.astype(F32)).astype(BF16)
    ss = (jnp.dot(hi, head_ones, preferred_element_type=F32)
          + jnp.dot(lo, head_ones, preferred_element_type=F32))
    return (x * lax.rsqrt(ss * (1.0 / SWA_HEAD_DIM) + EPS)) * w


def _pair_rope(x, cos, sin_signed, first_quarter):
    half = SWA_HEAD_DIM // 2
    rot = jnp.where(first_quarter, pltpu.roll(x, LANES - half, 1), pltpu.roll(x, half, 1))
    return x * cos + rot * sin_signed


def _swa_head(q_m, k_span, v_span, valid, sink, o_ref, lanes, needs_roll):
    sc = lax.dot_general(q_m, k_span, NT_DIMS, preferred_element_type=F32)
    yield
    sc = jnp.where(valid, sc, -1e30)
    mx = jnp.maximum(jnp.max(sc, axis=-1, keepdims=True), sink)
    yield
    p = jnp.exp(sc - mx)
    denom = jnp.sum(p, axis=-1, keepdims=True) + jnp.exp(sink - mx)
    yield
    pv = jnp.dot(p.astype(BF16), v_span, preferred_element_type=F32) / denom
    if needs_roll:
        pv = pltpu.roll(pv, SWA_HEAD_DIM, 1)
    o_ref[:, lanes] = pv[:, lanes.start % LANES:(lanes.start % LANES) + SWA_HEAD_DIM].astype(BF16)


def _swa_kernel(q_ref, kc_ref, kn_ref, vp_ref, vc_ref, vn_ref,
                cc_ref, sc_ref, cn_ref, sn_ref,
                qw_ref, kw_ref, sink_ref, o_ref, ks_ref):
    i = pl.program_id(0)
    n = pl.num_programs(0)
    batch = q_ref.shape[0]
    lane = lax.broadcasted_iota(jnp.int32, (TILE, LANES), 1)
    lo_half = lane < SWA_HEAD_DIM
    first_quarter = (lane % SWA_HEAD_DIM) < (SWA_HEAD_DIM // 2)

    sub = lax.broadcasted_iota(jnp.int32, (LANES, LANES), 0)
    head_ones = ((sub // SWA_HEAD_DIM) == (lane // SWA_HEAD_DIM)).astype(BF16)

    def prep(x, w, cos_ref, sin_ref):
        return _pair_rope(_pair_rms(x, w, head_ones), cos_ref[...], sin_ref[...], first_quarter)

    kw = kw_ref[...]

    @pl.when(i == 0)
    def _():
        for b in range(batch):
            ks_ref[3 * b] = jnp.zeros((TILE, LANES), BF16)
            ks_ref[3 * b + 1] = prep(kc_ref[b].astype(F32), kw, cc_ref, sc_ref).astype(BF16)

    @pl.when(i > 0)
    def _():
        for b in range(batch):
            ks_ref[3 * b] = ks_ref[3 * b + 1]
            ks_ref[3 * b + 1] = ks_ref[3 * b + 2]

    for b in range(batch):
        ks_ref[3 * b + 2] = prep(kn_ref[b].astype(F32), kw, cn_ref, sn_ref).astype(BF16)
    span = 3 * TILE
    r = lax.broadcasted_iota(jnp.int32, (TILE, span), 0)
    c = lax.broadcasted_iota(jnp.int32, (TILE, span), 1)
    valid = (jnp.abs(c - WINDOW - r) <= WINDOW)
    valid = valid & ((c >= TILE) | (i > 0)) & ((c < 2 * TILE) | (i < n - 1))
    group = SWA_Q_HEADS // SWA_KV_HEADS
    qw = qw_ref[...]
    scale = SWA_HEAD_DIM ** -0.5
    heads = []
    for b in range(batch):
        k_span = ks_ref[3 * b:3 * b + 3].reshape(3 * TILE, LANES)
        v_span = jnp.concatenate([vp_ref[b], vc_ref[b], vn_ref[b]], axis=0)
        for j in range(SWA_Q_HEADS // 2):
            q_pair = prep(q_ref[b, :, j * LANES:(j + 1) * LANES].astype(F32), qw, cc_ref, sc_ref) * scale
            q_swap = pltpu.roll(q_pair, SWA_HEAD_DIM, 1)
            for half in range(2):
                hq = 2 * j + half
                g = hq // group
                q_al = q_pair if half == g else q_swap
                q_m = jnp.where(lo_half if g == 0 else jnp.logical_not(lo_half), q_al, 0.0).astype(BF16)
                lanes = slice(hq * SWA_HEAD_DIM, (hq + 1) * SWA_HEAD_DIM)
                heads.append(_swa_head(q_m, k_span, v_span, valid, sink_ref[hq], o_ref.at[b], lanes,
                                       half != g))
    while heads:
        alive = []
        for head in heads:
            try:
                next(head)
                alive.append(head)
            except StopIteration:
                pass
        heads = alive


def _swa(proj3d, cos_t, sin_t, qw, kw, sinks):
    b, s, _ = proj3d.shape
    n_tiles = s // TILE
    kblk = COL_SWK // LANES
    vblk = COL_SWV // LANES
    prev = lambda i: jnp.maximum(i - 1, 0)
    nxt = lambda i: jnp.minimum(i + 1, n_tiles - 1)

    def kv_spec(blk, f):
        return pl.BlockSpec((b, TILE, LANES), lambda i: (0, f(i), blk))

    def tab_spec(f):
        return pl.BlockSpec((TILE, LANES), lambda i: (f(i), 0))

    ident = lambda i: i
    row_p = pl.BlockSpec((1, LANES), lambda i: (0, 0))
    return pl.pallas_call(
        _swa_kernel,
        grid=(n_tiles,),
        in_specs=[
            pl.BlockSpec((b, TILE, SWA_WIDTH), lambda i: (0, i, COL_SWQ // SWA_WIDTH)),
            kv_spec(kblk, ident), kv_spec(kblk, nxt),
            kv_spec(vblk, prev), kv_spec(vblk, ident), kv_spec(vblk, nxt),
            tab_spec(ident), tab_spec(ident), tab_spec(nxt), tab_spec(nxt),
            row_p, row_p,
            pl.BlockSpec(memory_space=pltpu.SMEM),
        ],
        out_specs=pl.BlockSpec((b, TILE, SWA_WIDTH), lambda i: (0, i, 0)),
        out_shape=jax.ShapeDtypeStruct((b, s, SWA_WIDTH), BF16),
        scratch_shapes=[pltpu.VMEM((3 * b, TILE, LANES), BF16)],
        compiler_params=pltpu.CompilerParams(
            dimension_semantics=("arbitrary",), vmem_limit_bytes=VMEM_LIMIT),
        name="swa",
    )(proj3d, proj3d, proj3d, proj3d, proj3d, proj3d,
      cos_t, sin_t, cos_t, sin_t, qw, kw, sinks)


def _out_kernel(of_ref, ob_ref, z_ref, sw_ref, swz_ref, x_ref, onw_ref, wo_ref, y_ref):
    dn = of_ref[...].astype(F32) + ob_ref[...].astype(F32)
    onw = onw_ref[...]
    parts = []
    for h in range(DN_HEADS):
        blk = dn[:, h * LANES:(h + 1) * LANES]
        ms = jnp.mean(blk * blk, axis=-1, keepdims=True)
        parts.append((blk * lax.rsqrt(ms + EPS)) * onw)
    dn_n = jnp.concatenate(parts, axis=1) * _silu(z_ref[...].astype(F32))
    sw = sw_ref[...].astype(F32) * _silu(swz_ref[...].astype(F32))
    mix = jnp.concatenate([dn_n, sw], axis=1).astype(BF16)
    y_ref[...] = x_ref[...] + jnp.dot(mix, wo_ref[...], preferred_element_type=F32)


def _out_projection(o_f, o_b, proj2d, sw, x2d, out_norm_w, w_out):
    rows = x2d.shape[0]
    half = lambda i: (i, 0)
    return pl.pallas_call(
        _out_kernel,
        grid=(rows // PROJ_ROWS,),
        in_specs=[
            pl.BlockSpec((PROJ_ROWS, DN_WIDTH), half),
            pl.BlockSpec((PROJ_ROWS, DN_WIDTH), half),
            pl.BlockSpec((PROJ_ROWS, DN_WIDTH), lambda i: (i, COL_DNZ // DN_WIDTH)),
            pl.BlockSpec((PROJ_ROWS, SWA_WIDTH), half),
            pl.BlockSpec((PROJ_ROWS, SWA_WIDTH), lambda i: (i, COL_SWZ // SWA_WIDTH)),
            pl.BlockSpec((PROJ_ROWS, D_MODEL), half),
            pl.BlockSpec((1, LANES), lambda i: (0, 0)),
            pl.BlockSpec((DN_WIDTH + SWA_WIDTH, D_MODEL), lambda i: (0, 0)),
        ],
        out_specs=pl.BlockSpec((PROJ_ROWS, D_MODEL), half),
        out_shape=jax.ShapeDtypeStruct((rows, D_MODEL), F32),
        compiler_params=pltpu.CompilerParams(
            dimension_semantics=("arbitrary",), vmem_limit_bytes=VMEM_LIMIT),
        name="out_proj",
    )(o_f, o_b, proj2d, sw, proj2d, x2d, out_norm_w, w_out)


def _rope_tables(seq):
    inv_freq = ROPE_THETA ** (-jnp.arange(0, SWA_HEAD_DIM, 2, dtype=F32) / SWA_HEAD_DIM)
    ang = jnp.arange(seq, dtype=F32)[:, None] * inv_freq[None, :]
    half = SWA_HEAD_DIM // 2
    lane = jnp.arange(LANES)
    expand = (lane[None, :] % half == jnp.arange(half)[:, None]).astype(F32)
    sign = jnp.where(lane % SWA_HEAD_DIM < half, -1.0, 1.0).astype(F32)
    cos = jnp.dot(jnp.cos(ang), expand, precision=lax.Precision.HIGHEST)
    sin = jnp.dot(jnp.sin(ang), expand, precision=lax.Precision.HIGHEST)
    return cos, sin * sign[None, :]


def _layer(x, norm_w, w_in, conv_w, a_log, dt_bias, out_norm_w, q_norm_w, k_norm_w, sinks, w_out):
    b, s, _ = x.shape
    x2d = x.reshape(b * s, D_MODEL)
    o_q, o_k, o_v, o_z = 0, DN_WIDTH, 2 * DN_WIDTH, 3 * DN_WIDTH
    o_beta = 4 * DN_WIDTH
    o_decay = o_beta + N_GATE
    o_swq = o_decay + N_GATE
    o_swk = o_swq + SWA_WIDTH
    o_swv = o_swk + SWA_KV_WIDTH
    o_swz = o_swv + SWA_KV_WIDTH
    w_gate = w_in[:, o_beta:o_swq]
    w_perm = jnp.concatenate([
        w_in[:, o_q:o_beta],
        w_in[:, o_swq:o_swk], w_in[:, o_swz:o_swz + SWA_WIDTH],
        w_in[:, o_swk:o_swv], w_in[:, o_swv:o_swz]], axis=1).astype(BF16)
    w_gate_pad = jnp.concatenate(
        [w_gate, jnp.zeros((D_MODEL, LANES - 2 * N_GATE), w_in.dtype)], axis=1).astype(BF16)
    conv_pad = jnp.concatenate([conv_w, jnp.zeros((SUBLANES - DN_CONV, 3 * DN_WIDTH), conv_w.dtype)], axis=0)
    proj2d, gate_c, gate_t, q, k, v = _in_projection_conv(
        x2d, s, norm_w.reshape(1, D_MODEL), w_perm, w_gate_pad, w_gate.T.astype(BF16), conv_pad)
    proj3d = proj2d.reshape(b, s, PROJ_WIDTH - 3 * DN_WIDTH)
    q, k, v = (a.reshape(b, s, DN_WIDTH) for a in (q, k, v))

    a_flat = a_log.reshape(N_GATE).astype(F32)
    d_flat = dt_bias.reshape(N_GATE).astype(F32)
    zeros8 = jnp.zeros((N_GATE,), F32)
    pad_row = jnp.zeros((LANES - 2 * N_GATE,), F32)
    al_row = jnp.concatenate([zeros8, a_flat, pad_row]).reshape(1, LANES)
    dt_row = jnp.concatenate([zeros8, d_flat, pad_row]).reshape(1, LANES)
    al_col = jnp.concatenate([zeros8, a_flat]).reshape(2 * N_GATE, 1)
    dt_col = jnp.concatenate([zeros8, d_flat]).reshape(2 * N_GATE, 1)
    o_f, o_b = _dn_scan(q, k, v, gate_c.reshape(b, s, LANES), gate_t, al_row, dt_row, al_col, dt_col)

    cos_t, sin_t = _rope_tables(s)
    reps = LANES // SWA_HEAD_DIM
    sw = _swa(proj3d, cos_t, sin_t,
              jnp.tile(q_norm_w.astype(F32), reps).reshape(1, LANES),
              jnp.tile(k_norm_w.astype(F32), reps).reshape(1, LANES),
              sinks.astype(F32))

    y = _out_projection(o_f.reshape(b * s, DN_WIDTH), o_b.reshape(b * s, DN_WIDTH), proj2d,
                        sw.reshape(b * s, SWA_WIDTH), x2d,
                        out_norm_w.reshape(1, LANES).astype(F32), w_out.astype(BF16))
    return y.reshape(b, s, D_MODEL)


def kernel(x, norm_w, w_in, dn_conv_w, dn_a_log, dn_dt_bias, dn_out_norm_w,
           swa_q_norm_w, swa_k_norm_w, swa_sinks, w_out):
    for l in range(norm_w.shape[0]):
        x = _layer(x, norm_w[l], w_in[l], dn_conv_w[l], dn_a_log[l], dn_dt_bias[l], dn_out_norm_w[l],
                   swa_q_norm_w[l], swa_k_norm_w[l], swa_sinks[l], w_out[l])
    return x
```

```python
import functools

import jax
import jax.numpy as jnp
from jax import lax
from jax.experimental import pallas as pl
from jax.experimental.pallas import tpu as pltpu

D_MODEL = 1024
DN_HEADS = 4
DN_HEAD_DIM = 128
DN_WIDTH = DN_HEADS * DN_HEAD_DIM
DN_CONV = 5
N_DIR = 2
N_GATE = N_DIR * DN_HEADS
SWA_Q_HEADS = 8
SWA_KV_HEADS = 2
SWA_HEAD_DIM = 64
SWA_WIDTH = SWA_Q_HEADS * SWA_HEAD_DIM
SWA_KV_WIDTH = SWA_KV_HEADS * SWA_HEAD_DIM
WINDOW = 128
ROPE_THETA = 10000.0
EPS = 1e-6

LANES = 128
SUBLANES = 8
TILE = 128
PROJ_ROWS = 512
VMEM_LIMIT = 48 * 1024 * 1024

COL_DNZ = 0
COL_SWQ = COL_DNZ + DN_WIDTH
COL_SWZ = COL_SWQ + SWA_WIDTH
COL_SWK = COL_SWZ + SWA_WIDTH
COL_SWV = COL_SWK + SWA_KV_WIDTH
PROJ_WIDTH = 3 * DN_WIDTH + COL_SWV + SWA_KV_WIDTH
HALO = 16

F32 = jnp.float32
BF16 = jnp.bfloat16
NT_DIMS = (((1,), (1,)), ((), ()))
TN_DIMS = (((0,), (0,)), ((), ()))


def _silu(x):
    return x * jax.nn.sigmoid(x)


def _softplus(x):
    return jnp.maximum(x, 0.0) + jnp.log1p(jnp.exp(-jnp.abs(x)))


def _dot(a, b):
    return jnp.dot(a.astype(BF16), b.astype(BF16), preferred_element_type=F32)


def _dot_nt(a, b):
    return lax.dot_general(a.astype(BF16), b.astype(BF16), NT_DIMS, preferred_element_type=F32)


def _dot_tn(a, b):
    return lax.dot_general(a.astype(BF16), b.astype(BF16), TN_DIMS, preferred_element_type=F32)


def _proj_conv_kernel(tiles_per_seq, x_ref, nw_ref, w_ref, wg_ref, wgt_ref, cw_ref,
                      rest_ref, gc_ref, gt_ref, q_ref, k_ref, v_ref, buf_ref, new_ref, h_ref):
    i = pl.program_id(0)
    n_tiles = pl.num_programs(0) - 2
    rows = x_ref.shape[0]
    pad = (DN_CONV - 1) // 2
    qkv_w = 3 * DN_WIDTH

    @pl.when(i == 0)
    def _():
        buf_ref[...] = jnp.zeros_like(buf_ref)
        h_ref[...] = jnp.zeros_like(h_ref)

    h = h_ref[...]
    t = i - 1
    boundary = (t % tiles_per_seq == 0) | (t < 0) | (t >= n_tiles)
    head = jnp.dot(h[0:HALO, :], w_ref[:, 0:qkv_w], preferred_element_type=F32)
    buf_ref[HALO + rows:2 * HALO + rows, :] = jnp.where(boundary, 0.0, head)
    new_ref[...] = jnp.dot(h, w_ref[:, 0:qkv_w], preferred_element_type=F32)

    gc_ref[...] = jnp.dot(h, wg_ref[...], preferred_element_type=F32)
    gt_ref[...] = lax.dot_general(wgt_ref[...], h, NT_DIMS, preferred_element_type=F32)

    outs = (q_ref, k_ref, v_ref)
    rest_w = rest_ref.shape[1]
    chunk = 2 * LANES
    n_chunks = rest_w // chunk
    n_blocks = 3 * DN_HEADS
    for c in range(n_blocks):
        for r in range(c * n_chunks // n_blocks, (c + 1) * n_chunks // n_blocks):
            cols = slice(r * chunk, (r + 1) * chunk)
            rest_ref[:, cols] = jnp.dot(h, w_ref[:, qkv_w + r * chunk:qkv_w + (r + 1) * chunk],
                                        preferred_element_type=F32).astype(BF16)
        lo = c * LANES
        window = buf_ref[:, lo:lo + LANES]
        acc = None
        for j in range(DN_CONV):
            shifted = window if j == pad else pltpu.roll(window, (pad - j) % window.shape[0], 0)
            term = shifted[HALO:HALO + rows] * cw_ref[j:j + 1, lo:lo + LANES]
            acc = term if acc is None else acc + term
        y = _silu(acc)
        part, head = divmod(c, DN_HEADS)
        if part < 2:
            y = y * lax.rsqrt(jnp.sum(y * y, axis=-1, keepdims=True) + EPS)
        if part == 0:
            y = y * (DN_HEAD_DIM ** -0.5)
        outs[part][:, head * LANES:(head + 1) * LANES] = y.astype(BF16)

    buf_ref[0:HALO, :] = jnp.where(boundary, 0.0, buf_ref[rows:rows + HALO, :])
    buf_ref[HALO:HALO + rows, :] = new_ref[...]

    x = x_ref[...]
    ms = jnp.mean(x * x, axis=-1, keepdims=True)
    h_ref[...] = ((x * lax.rsqrt(ms + EPS)) * nw_ref[...]).astype(BF16)


def _in_projection_conv(x2d, seq, norm_w, w_perm, w_gate, w_gate_t, conv_w):
    rows = x2d.shape[0]
    n_tiles = rows // PROJ_ROWS
    qkv_w = 3 * DN_WIDTH
    rest_w = PROJ_WIDTH - qkv_w
    clamp = lambda t: jnp.clip(t, 0, n_tiles - 1)
    cur = lambda i: (clamp(i), 0)
    prev = lambda i: (clamp(i - 1), 0)
    prev_t = lambda i: (0, clamp(i - 1))
    prev2 = lambda i: (clamp(i - 2), 0)
    const = lambda i: (0, 0)
    qkv_spec = pl.BlockSpec((PROJ_ROWS, DN_WIDTH), prev2)
    qkv_sds = jax.ShapeDtypeStruct((rows, DN_WIDTH), BF16)
    return pl.pallas_call(
        functools.partial(_proj_conv_kernel, seq // PROJ_ROWS),
        grid=(n_tiles + 2,),
        in_specs=[
            pl.BlockSpec((PROJ_ROWS, D_MODEL), cur),
            pl.BlockSpec((1, D_MODEL), const),
            pl.BlockSpec((D_MODEL, PROJ_WIDTH), const),
            pl.BlockSpec((D_MODEL, LANES), const),
            pl.BlockSpec((2 * N_GATE, D_MODEL), const),
            pl.BlockSpec((SUBLANES, qkv_w), const),
        ],
        out_specs=[
            pl.BlockSpec((PROJ_ROWS, rest_w), prev),
            pl.BlockSpec((PROJ_ROWS, LANES), prev),
            pl.BlockSpec((2 * N_GATE, PROJ_ROWS), prev_t),
            qkv_spec, qkv_spec, qkv_spec,
        ],
        out_shape=[
            jax.ShapeDtypeStruct((rows, rest_w), BF16),
            jax.ShapeDtypeStruct((rows, LANES), F32),
            jax.ShapeDtypeStruct((2 * N_GATE, rows), F32),
            qkv_sds, qkv_sds, qkv_sds,
        ],
        scratch_shapes=[pltpu.VMEM((PROJ_ROWS + 2 * HALO, qkv_w), F32),
                        pltpu.VMEM((PROJ_ROWS, qkv_w), F32),
                        pltpu.VMEM((PROJ_ROWS, D_MODEL), BF16)],
        compiler_params=pltpu.CompilerParams(
            dimension_semantics=("arbitrary",), vmem_limit_bytes=VMEM_LIMIT),
        name="in_proj_conv",
    )(x2d, norm_w, w_perm, w_gate, w_gate_t, conv_w)


def _split3(x):
    x1 = x.astype(BF16)
    r1 = x - x1.astype(F32)
    x2 = r1.astype(BF16)
    x3 = (r1 - x2.astype(F32)).astype(BF16)
    return x1, x2, x3


def _exact_left(m_bf16, x):
    return sum(jnp.dot(m_bf16, p, preferred_element_type=F32) for p in _split3(x))


def _exact_right(x, m_bf16):
    return sum(jnp.dot(p, m_bf16, preferred_element_type=F32) for p in _split3(x))


def _tri_inverse_minus_eye(x0, row, col, lower):
    same = lambda s: (row // s) == (col // s)
    s = 2
    p = jnp.where(same(s), x0, 0.0)
    while s < TILE:
        m_off = jnp.where(same(2 * s) & jnp.logical_not(same(s)), -x0, 0.0)
        if s < SUBLANES:
            z = m_off + _dot(m_off, p)
            yield
            p = p - z - _dot(p, z)
            yield
        else:
            starts = range(s if lower else 0, TILE, 2 * s)
            take = lambda a: jnp.concatenate([a[r:r + s] for r in starts], axis=0)

            def put(rows, base):
                parts = [rows[(r // (2 * s)) * s:(r // (2 * s) + 1) * s] if r in starts else base[r:r + s]
                         for r in range(0, TILE, s)]
                return jnp.concatenate(parts, axis=0)

            m_rows = take(m_off)
            z_rows = m_rows + _dot(m_rows, p)
            yield
            new_rows = take(p) - z_rows - _dot(take(p), put(z_rows, jnp.zeros_like(p)))
            yield
            p = put(new_rows, p)
        s *= 2
    return p


def _dn_chain(q, k, v, beta, gcc, gcr, tot, incl, lower, diag, row, col, state_ref, o_ref, hs):
    decay = jnp.where(incl, jnp.exp(jnp.where(incl, gcc - gcr, 0.0)), 0.0)
    gram = _dot_nt(k, k)
    qk = _dot_nt(q, k)
    yield
    x0 = jnp.where(diag, 0.0, -(gram * decay * beta))
    p = yield from _tri_inverse_minus_eye(x0, row, col, lower)
    egc = jnp.exp(gcc)
    rhs = jnp.concatenate([v * beta, k * (beta * egc)], axis=1)
    uw = rhs + _dot(p, rhs)
    yield
    auw = _dot(qk * decay, uw)
    kt = _dot_tn(k * jnp.exp(tot - gcc), uw)
    yield
    p_q = q * egc - auw[:, TILE:]
    state = state_ref[...]
    res = _dot(jnp.concatenate([kt[:, TILE:], p_q], axis=0), state)
    yield
    o_ref[:, hs] = (res[TILE:] + auw[:, :TILE]).astype(BF16)
    state_ref[...] = jnp.exp(tot) * state - res[:TILE] + kt[:, :TILE]


def _dn_kernel(batch, *refs):
    n_in = 4 + batch
    fwd_refs, bwd_refs = refs[0:n_in], refs[n_in:2 * n_in]
    alr_ref, dtr_ref, alc_ref, dtc_ref, of_ref, ob_ref, s_ref = refs[2 * n_in:]

    @pl.when(pl.program_id(0) == 0)
    def _():
        s_ref[...] = jnp.zeros_like(s_ref)

    row = lax.broadcasted_iota(jnp.int32, (TILE, TILE), 0)
    col = lax.broadcasted_iota(jnp.int32, (TILE, TILE), 1)
    lower = (col <= row)
    upper = (col >= row)
    diag = (col == row)
    lower_bf = lower.astype(BF16)
    upper_bf = upper.astype(BF16)
    neg_a_row = -jnp.exp(alr_ref[...])
    neg_a_col = -jnp.exp(alc_ref[...])

    streams = ((fwd_refs, of_ref, lower, lower_bf, upper_bf, TILE - 1),
               (bwd_refs, ob_ref, upper, upper_bf, lower_bf, 0))
    chains = []
    for d, (in_refs, o_ref, incl, cum_left, cum_right, last) in enumerate(streams):
        q_ref, k_ref, v_ref, gc_ref = in_refs[0:4]
        for b in range(batch):
            gate_c = gc_ref[b]
            gate_r = in_refs[4 + b][...]
            beta_c = jax.nn.sigmoid(gate_c)
            g_c = neg_a_row * _softplus(gate_c + dtr_ref[...])
            g_r = neg_a_col * _softplus(gate_r + dtc_ref[...])
            cum_c = _exact_left(cum_left, g_c)
            tot_c = cum_c[last:last + 1, :]
            cum_r = _exact_right(g_r, cum_right)
            for h in range(DN_HEADS):
                cb = d * DN_HEADS + h
                cg = N_GATE + cb
                hs = slice(h * LANES, (h + 1) * LANES)
                chains.append(_dn_chain(
                    q_ref[b, :, hs].astype(F32), k_ref[b, :, hs].astype(F32), v_ref[b, :, hs].astype(F32),
                    beta_c[:, cb:cb + 1], cum_c[:, cg:cg + 1], cum_r[cg:cg + 1, :], tot_c[:, cg:cg + 1],
                    incl, d == 0, diag, row, col, s_ref.at[b * N_GATE + cb], o_ref.at[b], hs))
    while chains:
        alive = []
        for chain in chains:
            try:
                next(chain)
                alive.append(chain)
            except StopIteration:
                pass
        chains = alive


def _dn_scan(q, k, v, gate_c, gate_t, al_row, dt_row, al_col, dt_col):
    b, s, _ = q.shape
    n_tiles = s // TILE
    fwd = lambda i: (0, i, 0)
    bwd = lambda i: (0, n_tiles - 1 - i, 0)
    const = lambda i: (0, 0)
    qkv_f = pl.BlockSpec((b, TILE, DN_WIDTH), fwd)
    qkv_b = pl.BlockSpec((b, TILE, DN_WIDTH), bwd)
    gc_f = pl.BlockSpec((b, TILE, LANES), fwd)
    gc_b = pl.BlockSpec((b, TILE, LANES), bwd)
    gr_f = [pl.BlockSpec((2 * N_GATE, TILE), functools.partial(lambda bb, i: (0, bb * n_tiles + i), bb))
            for bb in range(b)]
    gr_b = [pl.BlockSpec((2 * N_GATE, TILE),
                         functools.partial(lambda bb, i: (0, bb * n_tiles + n_tiles - 1 - i), bb))
            for bb in range(b)]
    row_p = pl.BlockSpec((1, LANES), const)
    col_p = pl.BlockSpec((2 * N_GATE, 1), const)
    out_sds = jax.ShapeDtypeStruct((b, s, DN_WIDTH), BF16)
    return pl.pallas_call(
        functools.partial(_dn_kernel, b),
        grid=(n_tiles,),
        in_specs=[qkv_f, qkv_f, qkv_f, gc_f, *gr_f,
                  qkv_b, qkv_b, qkv_b, gc_b, *gr_b,
                  row_p, row_p, col_p, col_p],
        out_specs=[qkv_f, qkv_b],
        out_shape=[out_sds, out_sds],
        scratch_shapes=[pltpu.VMEM((b * N_GATE, DN_HEAD_DIM, DN_HEAD_DIM), F32)],
        compiler_params=pltpu.CompilerParams(
            dimension_semantics=("arbitrary",), vmem_limit_bytes=VMEM_LIMIT),
        name="dn_scan",
    )(q, k, v, gate_c, *([gate_t] * b), q, k, v, gate_c, *([gate_t] * b), al_row, dt_row, al_col, dt_col)


def _pair_rms(x, w, head_ones):
    x2 = x * x
    hi = x2.astype(BF16)
    lo = (x2 - hi.astype(F32)).astype(BF16)
    ss = (jnp.dot(hi, head_ones, preferred_element_type=F32)
          + jnp.dot(lo, head_ones, preferred_element_type=F32))
    return (x * lax.rsqrt(ss * (1.0 / SWA_HEAD_DIM) + EPS)) * w


def _pair_rope(x, cos, sin_signed, first_quarter):
    half = SWA_HEAD_DIM // 2
    rot = jnp.where(first_quarter, pltpu.roll(x, LANES - half, 1), pltpu.roll(x, half, 1))
    return x * cos + rot * sin_signed


def _swa_head(q_m, k_span, v_span, valid, sink, o_ref, lanes, needs_roll):
    sc = lax.dot_general(q_m, k_span, NT_DIMS, preferred_element_type=F32)
    yield
    sc = jnp.where(valid, sc, -1e30)
    mx = jnp.maximum(jnp.max(sc, axis=-1, keepdims=True), sink)
    yield
    p = jnp.exp(sc - mx)
    denom = jnp.sum(p, axis=-1, keepdims=True) + jnp.exp(sink - mx)
    yield
    pv = jnp.dot(p.astype(BF16), v_span, preferred_element_type=F32) / denom
    if needs_roll:
        pv = pltpu.roll(pv, SWA_HEAD_DIM, 1)
    o_ref[:, lanes] = pv[:, lanes.start % LANES:(lanes.start % LANES) + SWA_HEAD_DIM].astype(BF16)


def _swa_kernel(q_ref, kc_ref, kn_ref, vp_ref, vc_ref, vn_ref,
                cc_ref, sc_ref, cn_ref, sn_ref,
                qw_ref, kw_ref, sink_ref, o_ref, ks_ref):
    i = pl.program_id(0)
    n = pl.num_programs(0)
    batch = q_ref.shape[0]
    lane = lax.broadcasted_iota(jnp.int32, (TILE, LANES), 1)
    lo_half = lane < SWA_HEAD_DIM
    first_quarter = (lane % SWA_HEAD_DIM) < (SWA_HEAD_DIM // 2)

    sub = lax.broadcasted_iota(jnp.int32, (LANES, LANES), 0)
    head_ones = ((sub // SWA_HEAD_DIM) == (lane // SWA_HEAD_DIM)).astype(BF16)

    def prep(x, w, cos_ref, sin_ref):
        return _pair_rope(_pair_rms(x, w, head_ones), cos_ref[...], sin_ref[...], first_quarter)

    kw = kw_ref[...]

    @pl.when(i == 0)
    def _():
        for b in range(batch):
            ks_ref[3 * b] = jnp.zeros((TILE, LANES), BF16)
            ks_ref[3 * b + 1] = prep(kc_ref[b].astype(F32), kw, cc_ref, sc_ref).astype(BF16)

    @pl.when(i > 0)
    def _():
        for b in range(batch):
            ks_ref[3 * b] = ks_ref[3 * b + 1]
            ks_ref[3 * b + 1] = ks_ref[3 * b + 2]

    for b in range(batch):
        ks_ref[3 * b + 2] = prep(kn_ref[b].astype(F32), kw, cn_ref, sn_ref).astype(BF16)
    span = 3 * TILE
    r = lax.broadcasted_iota(jnp.int32, (TILE, span), 0)
    c = lax.broadcasted_iota(jnp.int32, (TILE, span), 1)
    valid = (jnp.abs(c - WINDOW - r) <= WINDOW)
    valid = valid & ((c >= TILE) | (i > 0)) & ((c < 2 * TILE) | (i < n - 1))
    group = SWA_Q_HEADS // SWA_KV_HEADS
    qw = qw_ref[...]
    scale = SWA_HEAD_DIM ** -0.5
    heads = []
    for b in range(batch):
        k_span = ks_ref[3 * b:3 * b + 3].reshape(3 * TILE, LANES)
        v_span = jnp.concatenate([vp_ref[b], vc_ref[b], vn_ref[b]], axis=0)
        for j in range(SWA_Q_HEADS // 2):
            q_pair = prep(q_ref[b, :, j * LANES:(j + 1) * LANES].astype(F32), qw, cc_ref, sc_ref) * scale
            q_swap = pltpu.roll(q_pair, SWA_HEAD_DIM, 1)
            for half in range(2):
                hq = 2 * j + half
                g = hq // group
                q_al = q_pair if half == g else q_swap
                q_m = jnp.where(lo_half if g == 0 else jnp.logical_not(lo_half), q_al, 0.0).astype(BF16)
                lanes = slice(hq * SWA_HEAD_DIM, (hq + 1) * SWA_HEAD_DIM)
                heads.append(_swa_head(q_m, k_span, v_span, valid, sink_ref[hq], o_ref.at[b], lanes,
                                       half != g))
    while heads:
        alive = []
        for head in heads:
            try:
                next(head)
                alive.append(head)
            except StopIteration:
                pass
        heads = alive


def _swa(proj3d, cos_t, sin_t, qw, kw, sinks):
    b, s, _ = proj3d.shape
    n_tiles = s // TILE
    kblk = COL_SWK // LANES
    vblk = COL_SWV // LANES
    prev = lambda i: jnp.maximum(i - 1, 0)
    nxt = lambda i: jnp.minimum(i + 1, n_tiles - 1)

    def kv_spec(blk, f):
        return pl.BlockSpec((b, TILE, LANES), lambda i: (0, f(i), blk))

    def tab_spec(f):
        return pl.BlockSpec((TILE, LANES), lambda i: (f(i), 0))

    ident = lambda i: i
    row_p = pl.BlockSpec((1, LANES), lambda i: (0, 0))
    return pl.pallas_call(
        _swa_kernel,
        grid=(n_tiles,),
        in_specs=[
            pl.BlockSpec((b, TILE, SWA_WIDTH), lambda i: (0, i, COL_SWQ // SWA_WIDTH)),
            kv_spec(kblk, ident), kv_spec(kblk, nxt),
            kv_spec(vblk, prev), kv_spec(vblk, ident), kv_spec(vblk, nxt),
            tab_spec(ident), tab_spec(ident), tab_spec(nxt), tab_spec(nxt),
            row_p, row_p,
            pl.BlockSpec(memory_space=pltpu.SMEM),
        ],
        out_specs=pl.BlockSpec((b, TILE, SWA_WIDTH), lambda i: (0, i, 0)),
        out_shape=jax.ShapeDtypeStruct((b, s, SWA_WIDTH), BF16),
        scratch_shapes=[pltpu.VMEM((3 * b, TILE, LANES), BF16)],
        compiler_params=pltpu.CompilerParams(
            dimension_semantics=("arbitrary",), vmem_limit_bytes=VMEM_LIMIT),
        name="swa",
    )(proj3d, proj3d, proj3d, proj3d, proj3d, proj3d,
      cos_t, sin_t, cos_t, sin_t, qw, kw, sinks)


def _out_kernel(of_ref, ob_ref, z_ref, sw_ref, swz_ref, x_ref, onw_ref, wo_ref, y_ref):
    dn = of_ref[...].astype(F32) + ob_ref[...].astype(F32)
    onw = onw_ref[...]
    parts = []
    for h in range(DN_HEADS):
        blk = dn[:, h * LANES:(h + 1) * LANES]
        ms = jnp.mean(blk * blk, axis=-1, keepdims=True)
        parts.append((blk * lax.rsqrt(ms + EPS)) * onw)
    dn_n = jnp.concatenate(parts, axis=1) * _silu(z_ref[...].astype(F32))
    sw = sw_ref[...].astype(F32) * _silu(swz_ref[...].astype(F32))
    mix = jnp.concatenate([dn_n, sw], axis=1).astype(BF16)
    y_ref[...] = x_ref[...] + jnp.dot(mix, wo_ref[...], preferred_element_type=F32)


def _out_projection(o_f, o_b, proj2d, sw, x2d, out_norm_w, w_out):
    rows = x2d.shape[0]
    half = lambda i: (i, 0)
    return pl.pallas_call(
        _out_kernel,
        grid=(rows // PROJ_ROWS,),
        in_specs=[
            pl.BlockSpec((PROJ_ROWS, DN_WIDTH), half),
            pl.BlockSpec((PROJ_ROWS, DN_WIDTH), half),
            pl.BlockSpec((PROJ_ROWS, DN_WIDTH), lambda i: (i, COL_DNZ // DN_WIDTH)),
            pl.BlockSpec((PROJ_ROWS, SWA_WIDTH), half),
            pl.BlockSpec((PROJ_ROWS, SWA_WIDTH), lambda i: (i, COL_SWZ // SWA_WIDTH)),
            pl.BlockSpec((PROJ_ROWS, D_MODEL), half),
            pl.BlockSpec((1, LANES), lambda i: (0, 0)),
            pl.BlockSpec((DN_WIDTH + SWA_WIDTH, D_MODEL), lambda i: (0, 0)),
        ],
        out_specs=pl.BlockSpec((PROJ_ROWS, D_MODEL), half),
        out_shape=jax.ShapeDtypeStruct((rows, D_MODEL), F32),
        compiler_params=pltpu.CompilerParams(
            dimension_semantics=("arbitrary",), vmem_limit_bytes=VMEM_LIMIT),
        name="out_proj",
    )(o_f, o_b, proj2d, sw, proj2d, x2d, out_norm_w, w_out)


def _rope_tables(seq):
    inv_freq = ROPE_THETA ** (-jnp.arange(0, SWA_HEAD_DIM, 2, dtype=F32) / SWA_HEAD_DIM)
    ang = jnp.arange(seq, dtype=F32)[:, None] * inv_freq[None, :]
    half = SWA_HEAD_DIM // 2
    lane = jnp.arange(LANES)
    expand = (lane[None, :] % half == jnp.arange(half)[:, None]).astype(F32)
    sign = jnp.where(lane % SWA_HEAD_DIM < half, -1.0, 1.0).astype(F32)
    cos = jnp.dot(jnp.cos(ang), expand, precision=lax.Precision.HIGHEST)
    sin = jnp.dot(jnp.sin(ang), expand, precision=lax.Precision.HIGHEST)
    return cos, sin * sign[None, :]


def _layer(x, norm_w, w_in, conv_w, a_log, dt_bias, out_norm_w, q_norm_w, k_norm_w, sinks, w_out):
    b, s, _ = x.shape
    x2d = x.reshape(b * s, D_MODEL)
    o_q, o_k, o_v, o_z = 0, DN_WIDTH, 2 * DN_WIDTH, 3 * DN_WIDTH
    o_beta = 4 * DN_WIDTH
    o_decay = o_beta + N_GATE
    o_swq = o_decay + N_GATE
    o_swk = o_swq + SWA_WIDTH
    o_swv = o_swk + SWA_KV_WIDTH
    o_swz = o_swv + SWA_KV_WIDTH
    w_gate = w_in[:, o_beta:o_swq]
    w_perm = jnp.concatenate([
        w_in[:, o_q:o_beta],
        w_in[:, o_swq:o_swk], w_in[:, o_swz:o_swz + SWA_WIDTH],
        w_in[:, o_swk:o_swv], w_in[:, o_swv:o_swz]], axis=1).astype(BF16)
    w_gate_pad = jnp.concatenate(
        [w_gate, jnp.zeros((D_MODEL, LANES - 2 * N_GATE), w_in.dtype)], axis=1).astype(BF16)
    conv_pad = jnp.concatenate([conv_w, jnp.zeros((SUBLANES - DN_CONV, 3 * DN_WIDTH), conv_w.dtype)], axis=0)
    proj2d, gate_c, gate_t, q, k, v = _in_projection_conv(
        x2d, s, norm_w.reshape(1, D_MODEL), w_perm, w_gate_pad, w_gate.T.astype(BF16), conv_pad)
    proj3d = proj2d.reshape(b, s, PROJ_WIDTH - 3 * DN_WIDTH)
    q, k, v = (a.reshape(b, s, DN_WIDTH) for a in (q, k, v))

    a_flat = a_log.reshape(N_GATE).astype(F32)
    d_flat = dt_bias.reshape(N_GATE).astype(F32)
    zeros8 = jnp.zeros((N_GATE,), F32)
    pad_row = jnp.zeros((LANES - 2 * N_GATE,), F32)
    al_row = jnp.concatenate([zeros8, a_flat, pad_row]).reshape(1, LANES)
    dt_row = jnp.concatenate([zeros8, d_flat, pad_row]).reshape(1, LANES)
    al_col = jnp.concatenate([zeros8, a_flat]).reshape(2 * N_GATE, 1)
    dt_col = jnp.concatenate([zeros8, d_flat]).reshape(2 * N_GATE, 1)
    o_f, o_b = _dn_scan(q, k, v, gate_c.reshape(b, s, LANES), gate_t, al_row, dt_row, al_col, dt_col)

    cos_t, sin_t = _rope_tables(s)
    reps = LANES // SWA_HEAD_DIM
    sw = _swa(proj3d, cos_t, sin_t,
              jnp.tile(q_norm_w.astype(F32), reps).reshape(1, LANES),
              jnp.tile(k_norm_w.astype(F32), reps).reshape(1, LANES),
              sinks.astype(F32))

    y = _out_projection(o_f.reshape(b * s, DN_WIDTH), o_b.reshape(b * s, DN_WIDTH), proj2d,
                        sw.reshape(b * s, SWA_WIDTH), x2d,
                        out_norm_w.reshape(1, LANES).astype(F32), w_out.astype(BF16))
    return y.reshape(b, s, D_MODEL)


def kernel(x, norm_w, w_in, dn_conv_w, dn_a_log, dn_dt_bias, dn_out_norm_w,
           swa_q_norm_w, swa_k_norm_w, swa_sinks, w_out):
    for l in range(norm_w.shape[0]):
        x = _layer(x, norm_w[l], w_in[l], dn_conv_w[l], dn_a_log[l], dn_dt_bias[l], dn_out_norm_w[l],
                   swa_q_norm_w[l], swa_k_norm_w[l], swa_sinks[l], w_out[l])
    return x
```

```python
import functools

import jax
import jax.numpy as jnp
from jax import lax
from jax.experimental import pallas as pl
from jax.experimental.pallas import tpu as pltpu

D_MODEL = 1024
DN_HEADS = 4
DN_HEAD_DIM = 128
DN_WIDTH = DN_HEADS * DN_HEAD_DIM
DN_CONV = 5
N_DIR = 2
N_GATE = N_DIR * DN_HEADS
SWA_Q_HEADS = 8
SWA_KV_HEADS = 2
SWA_HEAD_DIM = 64
SWA_WIDTH = SWA_Q_HEADS * SWA_HEAD_DIM
SWA_KV_WIDTH = SWA_KV_HEADS * SWA_HEAD_DIM
WINDOW = 128
ROPE_THETA = 10000.0
EPS = 1e-6

LANES = 128
SUBLANES = 8
TILE = 128
PROJ_ROWS = 512
VMEM_LIMIT = 48 * 1024 * 1024

COL_DNZ = 0
COL_SWQ = COL_DNZ + DN_WIDTH
COL_SWZ = COL_SWQ + SWA_WIDTH
COL_SWK = COL_SWZ + SWA_WIDTH
COL_SWV = COL_SWK + SWA_KV_WIDTH
PROJ_WIDTH = 3 * DN_WIDTH + COL_SWV + SWA_KV_WIDTH
HALO = 16

F32 = jnp.float32
BF16 = jnp.bfloat16
NT_DIMS = (((1,), (1,)), ((), ()))
TN_DIMS = (((0,), (0,)), ((), ()))


def _silu(x):
    return x * jax.nn.sigmoid(x)


def _softplus(x):
    return jnp.maximum(x, 0.0) + jnp.log1p(jnp.exp(-jnp.abs(x)))


def _dot(a, b):
    return jnp.dot(a.astype(BF16), b.astype(BF16), preferred_element_type=F32)


def _dot_nt(a, b):
    return lax.dot_general(a.astype(BF16), b.astype(BF16), NT_DIMS, preferred_element_type=F32)


def _dot_tn(a, b):
    return lax.dot_general(a.astype(BF16), b.astype(BF16), TN_DIMS, preferred_element_type=F32)


def _proj_conv_kernel(tiles_per_seq, x_ref, nw_ref, w_ref, wg_ref, wgt_ref, cw_ref,
                      rest_ref, gc_ref, gt_ref, q_ref, k_ref, v_ref, buf_ref, new_ref):
    i = pl.program_id(0)
    n_tiles = pl.num_programs(0) - 1
    rows = x_ref.shape[0]
    pad = (DN_CONV - 1) // 2
    qkv_w = 3 * DN_WIDTH

    @pl.when(i == 0)
    def _():
        buf_ref[...] = jnp.zeros_like(buf_ref)

    x = x_ref[...]
    ms = jnp.mean(x * x, axis=-1, keepdims=True)
    h = ((x * lax.rsqrt(ms + EPS)) * nw_ref[...]).astype(BF16)
    boundary = (i % tiles_per_seq == 0) | (i == n_tiles)
    head = jnp.dot(h[0:HALO, :], w_ref[:, 0:qkv_w], preferred_element_type=F32)
    buf_ref[HALO + rows:2 * HALO + rows, :] = jnp.where(boundary, 0.0, head)
    new_ref[...] = jnp.dot(h, w_ref[:, 0:qkv_w], preferred_element_type=F32)

    gc_ref[...] = jnp.dot(h, wg_ref[...], preferred_element_type=F32)
    gt_ref[...] = lax.dot_general(wgt_ref[...], h, NT_DIMS, preferred_element_type=F32)

    outs = (q_ref, k_ref, v_ref)
    rest_w = rest_ref.shape[1]
    chunk = 2 * LANES
    n_chunks = rest_w // chunk
    n_blocks = 3 * DN_HEADS
    for c in range(n_blocks):
        for r in range(c * n_chunks // n_blocks, (c + 1) * n_chunks // n_blocks):
            cols = slice(r * chunk, (r + 1) * chunk)
            rest_ref[:, cols] = jnp.dot(h, w_ref[:, qkv_w + r * chunk:qkv_w + (r + 1) * chunk],
                                        preferred_element_type=F32).astype(BF16)
        lo = c * LANES
        window = buf_ref[:, lo:lo + LANES]
        acc = None
        for j in range(DN_CONV):
            shifted = window if j == pad else pltpu.roll(window, (pad - j) % window.shape[0], 0)
            term = shifted[HALO:HALO + rows] * cw_ref[j:j + 1, lo:lo + LANES]
            acc = term if acc is None else acc + term
        y = _silu(acc)
        part, head = divmod(c, DN_HEADS)
        if part < 2:
            y = y * lax.rsqrt(jnp.sum(y * y, axis=-1, keepdims=True) + EPS)
        if part == 0:
            y = y * (DN_HEAD_DIM ** -0.5)
        outs[part][:, head * LANES:(head + 1) * LANES] = y.astype(BF16)

    buf_ref[0:HALO, :] = jnp.where(boundary, 0.0, buf_ref[rows:rows + HALO, :])
    buf_ref[HALO:HALO + rows, :] = new_ref[...]


def _in_projection_conv(x2d, seq, norm_w, w_perm, w_gate, w_gate_t, conv_w):
    rows = x2d.shape[0]
    n_tiles = rows // PROJ_ROWS
    qkv_w = 3 * DN_WIDTH
    rest_w = PROJ_WIDTH - qkv_w
    clamp = lambda t: jnp.clip(t, 0, n_tiles - 1)
    cur = lambda i: (clamp(i), 0)
    cur_t = lambda i: (0, clamp(i))
    prev = lambda i: (clamp(i - 1), 0)
    const = lambda i: (0, 0)
    qkv_spec = pl.BlockSpec((PROJ_ROWS, DN_WIDTH), prev)
    qkv_sds = jax.ShapeDtypeStruct((rows, DN_WIDTH), BF16)
    return pl.pallas_call(
        functools.partial(_proj_conv_kernel, seq // PROJ_ROWS),
        grid=(n_tiles + 1,),
        in_specs=[
            pl.BlockSpec((PROJ_ROWS, D_MODEL), cur),
            pl.BlockSpec((1, D_MODEL), const),
            pl.BlockSpec((D_MODEL, PROJ_WIDTH), const),
            pl.BlockSpec((D_MODEL, LANES), const),
            pl.BlockSpec((2 * N_GATE, D_MODEL), const),
            pl.BlockSpec((SUBLANES, qkv_w), const),
        ],
        out_specs=[
            pl.BlockSpec((PROJ_ROWS, rest_w), cur),
            pl.BlockSpec((PROJ_ROWS, LANES), cur),
            pl.BlockSpec((2 * N_GATE, PROJ_ROWS), cur_t),
            qkv_spec, qkv_spec, qkv_spec,
        ],
        out_shape=[
            jax.ShapeDtypeStruct((rows, rest_w), BF16),
            jax.ShapeDtypeStruct((rows, LANES), F32),
            jax.ShapeDtypeStruct((2 * N_GATE, rows), F32),
            qkv_sds, qkv_sds, qkv_sds,
        ],
        scratch_shapes=[pltpu.VMEM((PROJ_ROWS + 2 * HALO, qkv_w), F32),
                        pltpu.VMEM((PROJ_ROWS, qkv_w), F32)],
        compiler_params=pltpu.CompilerParams(
            dimension_semantics=("arbitrary",), vmem_limit_bytes=VMEM_LIMIT),
        name="in_proj_conv",
    )(x2d, norm_w, w_perm, w_gate, w_gate_t, conv_w)


def _split3(x):
    x1 = x.astype(BF16)
    r1 = x - x1.astype(F32)
    x2 = r1.astype(BF16)
    x3 = (r1 - x2.astype(F32)).astype(BF16)
    return x1, x2, x3


def _exact_left(m_bf16, x):
    return sum(jnp.dot(m_bf16, p, preferred_element_type=F32) for p in _split3(x))


def _exact_right(x, m_bf16):
    return sum(jnp.dot(p, m_bf16, preferred_element_type=F32) for p in _split3(x))


def _tri_inverse_minus_eye(x0, row, col, lower):
    same = lambda s: (row // s) == (col // s)
    s = 2
    p = jnp.where(same(s), x0, 0.0)
    while s < TILE:
        m_off = jnp.where(same(2 * s) & jnp.logical_not(same(s)), -x0, 0.0)
        if s < SUBLANES:
            z = m_off + _dot(m_off, p)
            yield
            p = p - z - _dot(p, z)
            yield
        else:
            starts = range(s if lower else 0, TILE, 2 * s)
            take = lambda a: jnp.concatenate([a[r:r + s] for r in starts], axis=0)

            def put(rows, base):
                parts = [rows[(r // (2 * s)) * s:(r // (2 * s) + 1) * s] if r in starts else base[r:r + s]
                         for r in range(0, TILE, s)]
                return jnp.concatenate(parts, axis=0)

            m_rows = take(m_off)
            z_rows = m_rows + _dot(m_rows, p)
            yield
            new_rows = take(p) - z_rows - _dot(take(p), put(z_rows, jnp.zeros_like(p)))
            yield
            p = put(new_rows, p)
        s *= 2
    return p


def _dn_chain(q, k, v, beta, gcc, gcr, tot, incl, lower, diag, row, col, state_ref, o_ref, hs):
    decay = jnp.where(incl, jnp.exp(jnp.where(incl, gcc - gcr, 0.0)), 0.0)
    gram = _dot_nt(k, k)
    qk = _dot_nt(q, k)
    yield
    x0 = jnp.where(diag, 0.0, -(gram * decay * beta))
    p = yield from _tri_inverse_minus_eye(x0, row, col, lower)
    egc = jnp.exp(gcc)
    rhs = jnp.concatenate([v * beta, k * (beta * egc)], axis=1)
    uw = rhs + _dot(p, rhs)
    yield
    auw = _dot(qk * decay, uw)
    kt = _dot_tn(k * jnp.exp(tot - gcc), uw)
    yield
    p_q = q * egc - auw[:, TILE:]
    state = state_ref[...]
    res = _dot(jnp.concatenate([kt[:, TILE:], p_q], axis=0), state)
    yield
    o_ref[:, hs] = (res[TILE:] + auw[:, :TILE]).astype(BF16)
    state_ref[...] = jnp.exp(tot) * state - res[:TILE] + kt[:, :TILE]


def _dn_kernel(batch, *refs):
    n_in = 4 + batch
    fwd_refs, bwd_refs = refs[0:n_in], refs[n_in:2 * n_in]
    alr_ref, dtr_ref, alc_ref, dtc_ref, of_ref, ob_ref, s_ref = refs[2 * n_in:]

    @pl.when(pl.program_id(0) == 0)
    def _():
        s_ref[...] = jnp.zeros_like(s_ref)

    row = lax.broadcasted_iota(jnp.int32, (TILE, TILE), 0)
    col = lax.broadcasted_iota(jnp.int32, (TILE, TILE), 1)
    lower = (col <= row)
    upper = (col >= row)
    diag = (col == row)
    lower_bf = lower.astype(BF16)
    upper_bf = upper.astype(BF16)
    neg_a_row = -jnp.exp(alr_ref[...])
    neg_a_col = -jnp.exp(alc_ref[...])

    streams = ((fwd_refs, of_ref, lower, lower_bf, upper_bf, TILE - 1),
               (bwd_refs, ob_ref, upper, upper_bf, lower_bf, 0))
    chains = []
    for d, (in_refs, o_ref, incl, cum_left, cum_right, last) in enumerate(streams):
        q_ref, k_ref, v_ref, gc_ref = in_refs[0:4]
        for b in range(batch):
            gate_c = gc_ref[b]
            gate_r = in_refs[4 + b][...]
            beta_c = jax.nn.sigmoid(gate_c)
            g_c = neg_a_row * _softplus(gate_c + dtr_ref[...])
            g_r = neg_a_col * _softplus(gate_r + dtc_ref[...])
            cum_c = _exact_left(cum_left, g_c)
            tot_c = cum_c[last:last + 1, :]
            cum_r = _exact_right(g_r, cum_right)
            for h in range(DN_HEADS):
                cb = d * DN_HEADS + h
                cg = N_GATE + cb
                hs = slice(h * LANES, (h + 1) * LANES)
                chains.append(_dn_chain(
                    q_ref[b, :, hs].astype(F32), k_ref[b, :, hs].astype(F32), v_ref[b, :, hs].astype(F32),
                    beta_c[:, cb:cb + 1], cum_c[:, cg:cg + 1], cum_r[cg:cg + 1, :], tot_c[:, cg:cg + 1],
                    incl, d == 0, diag, row, col, s_ref.at[b * N_GATE + cb], o_ref.at[b], hs))
    while chains:
        alive = []
        for chain in chains:
            try:
                next(chain)
                alive.append(chain)
            except StopIteration:
                pass
        chains = alive


def _dn_scan(q, k, v, gate_c, gate_t, al_row, dt_row, al_col, dt_col):
    b, s, _ = q.shape
    n_tiles = s // TILE
    fwd = lambda i: (0, i, 0)
    bwd = lambda i: (0, n_tiles - 1 - i, 0)
    const = lambda i: (0, 0)
    qkv_f = pl.BlockSpec((b, TILE, DN_WIDTH), fwd)
    qkv_b = pl.BlockSpec((b, TILE, DN_WIDTH), bwd)
    gc_f = pl.BlockSpec((b, TILE, LANES), fwd)
    gc_b = pl.BlockSpec((b, TILE, LANES), bwd)
    gr_f = [pl.BlockSpec((2 * N_GATE, TILE), functools.partial(lambda bb, i: (0, bb * n_tiles + i), bb))
            for bb in range(b)]
    gr_b = [pl.BlockSpec((2 * N_GATE, TILE),
                         functools.partial(lambda bb, i: (0, bb * n_tiles + n_tiles - 1 - i), bb))
            for bb in range(b)]
    row_p = pl.BlockSpec((1, LANES), const)
    col_p = pl.BlockSpec((2 * N_GATE, 1), const)
    out_sds = jax.ShapeDtypeStruct((b, s, DN_WIDTH), BF16)
    return pl.pallas_call(
        functools.partial(_dn_kernel, b),
        grid=(n_tiles,),
        in_specs=[qkv_f, qkv_f, qkv_f, gc_f, *gr_f,
                  qkv_b, qkv_b, qkv_b, gc_b, *gr_b,
                  row_p, row_p, col_p, col_p],
        out_specs=[qkv_f, qkv_b],
        out_shape=[out_sds, out_sds],
        scratch_shapes=[pltpu.VMEM((b * N_GATE, DN_HEAD_DIM, DN_HEAD_DIM), F32)],
        compiler_params=pltpu.CompilerParams(
            dimension_semantics=("arbitrary",), vmem_limit_bytes=VMEM_LIMIT),
        name="dn_scan",
    )(q, k, v, gate_c, *([gate_t] * b), q, k, v, gate_c, *([gate_t] * b), al_row, dt_row, al_col, dt_col)


def _pair_rms(x, w, head_ones):
    x2 = x * x
    hi = x2.astype(BF16)
    lo = (x2 - hi.astype(F32)).astype(BF16)
    ss = (jnp.dot(hi, head_ones, preferred_element_type=F32)
          + jnp.dot(lo, head_ones, preferred_element_type=F32))
    return (x * lax.rsqrt(ss * (1.0 / SWA_HEAD_DIM) + EPS)) * w


def _pair_rope(x, cos, sin_signed, first_quarter):
    half = SWA_HEAD_DIM // 2
    rot = jnp.where(first_quarter, pltpu.roll(x, LANES - half, 1), pltpu.roll(x, half, 1))
    return x * cos + rot * sin_signed


def _swa_head(q_m, k_span, v_span, valid, sink, o_ref, lanes, needs_roll):
    sc = lax.dot_general(q_m, k_span, NT_DIMS, preferred_element_type=F32)
    yield
    sc = jnp.where(valid, sc, -1e30)
    mx = jnp.maximum(jnp.max(sc, axis=-1, keepdims=True), sink)
    yield
    p = jnp.exp(sc - mx)
    denom = jnp.sum(p, axis=-1, keepdims=True) + jnp.exp(sink - mx)
    yield
    pv = jnp.dot(p.astype(BF16), v_span, preferred_element_type=F32) / denom
    if needs_roll:
        pv = pltpu.roll(pv, SWA_HEAD_DIM, 1)
    o_ref[:, lanes] = pv[:, lanes.start % LANES:(lanes.start % LANES) + SWA_HEAD_DIM].astype(BF16)


def _swa_kernel(q_ref, kc_ref, kn_ref, vp_ref, vc_ref, vn_ref,
                cc_ref, sc_ref, cn_ref, sn_ref,
                qw_ref, kw_ref, sink_ref, o_ref, ks_ref):
    i = pl.program_id(0)
    n = pl.num_programs(0)
    batch = q_ref.shape[0]
    lane = lax.broadcasted_iota(jnp.int32, (TILE, LANES), 1)
    lo_half = lane < SWA_HEAD_DIM
    first_quarter = (lane % SWA_HEAD_DIM) < (SWA_HEAD_DIM // 2)

    sub = lax.broadcasted_iota(jnp.int32, (LANES, LANES), 0)
    head_ones = ((sub // SWA_HEAD_DIM) == (lane // SWA_HEAD_DIM)).astype(BF16)

    def prep(x, w, cos_ref, sin_ref):
        return _pair_rope(_pair_rms(x, w, head_ones), cos_ref[...], sin_ref[...], first_quarter)

    kw = kw_ref[...]

    @pl.when(i == 0)
    def _():
        for b in range(batch):
            ks_ref[3 * b] = jnp.zeros((TILE, LANES), BF16)
            ks_ref[3 * b + 1] = prep(kc_ref[b].astype(F32), kw, cc_ref, sc_ref).astype(BF16)

    @pl.when(i > 0)
    def _():
        for b in range(batch):
            ks_ref[3 * b] = ks_ref[3 * b + 1]
            ks_ref[3 * b + 1] = ks_ref[3 * b + 2]

    for b in range(batch):
        ks_ref[3 * b + 2] = prep(kn_ref[b].astype(F32), kw, cn_ref, sn_ref).astype(BF16)
    span = 3 * TILE
    r = lax.broadcasted_iota(jnp.int32, (TILE, span), 0)
    c = lax.broadcasted_iota(jnp.int32, (TILE, span), 1)
    valid = (jnp.abs(c - WINDOW - r) <= WINDOW)
    valid = valid & ((c >= TILE) | (i > 0)) & ((c < 2 * TILE) | (i < n - 1))
    group = SWA_Q_HEADS // SWA_KV_HEADS
    qw = qw_ref[...]
    scale = SWA_HEAD_DIM ** -0.5
    heads = []
    for b in range(batch):
        k_span = ks_ref[3 * b:3 * b + 3].reshape(3 * TILE, LANES)
        v_span = jnp.concatenate([vp_ref[b], vc_ref[b], vn_ref[b]], axis=0)
        for j in range(SWA_Q_HEADS // 2):
            q_pair = prep(q_ref[b, :, j * LANES:(j + 1) * LANES].astype(F32), qw, cc_ref, sc_ref) * scale
            q_swap = pltpu.roll(q_pair, SWA_HEAD_DIM, 1)
            for half in range(2):
                hq = 2 * j + half
                g = hq // group
                q_al = q_pair if half == g else q_swap
                q_m = jnp.where(lo_half if g == 0 else jnp.logical_not(lo_half), q_al, 0.0).astype(BF16)
                lanes = slice(hq * SWA_HEAD_DIM, (hq + 1) * SWA_HEAD_DIM)
                heads.append(_swa_head(q_m, k_span, v_span, valid, sink_ref[hq], o_ref.at[b], lanes,
                                       half != g))
    while heads:
        alive = []
        for head in heads:
            try:
                next(head)
                alive.append(head)
            except StopIteration:
                pass
        heads = alive


def _swa(proj3d, cos_t, sin_t, qw, kw, sinks):
    b, s, _ = proj3d.shape
    n_tiles = s // TILE
    kblk = COL_SWK // LANES
    vblk = COL_SWV // LANES
    prev = lambda i: jnp.maximum(i - 1, 0)
    nxt = lambda i: jnp.minimum(i + 1, n_tiles - 1)

    def kv_spec(blk, f):
        return pl.BlockSpec((b, TILE, LANES), lambda i: (0, f(i), blk))

    def tab_spec(f):
        return pl.BlockSpec((TILE, LANES), lambda i: (f(i), 0))

    ident = lambda i: i
    row_p = pl.BlockSpec((1, LANES), lambda i: (0, 0))
    return pl.pallas_call(
        _swa_kernel,
        grid=(n_tiles,),
        in_specs=[
            pl.BlockSpec((b, TILE, SWA_WIDTH), lambda i: (0, i, COL_SWQ // SWA_WIDTH)),
            kv_spec(kblk, ident), kv_spec(kblk, nxt),
            kv_spec(vblk, prev), kv_spec(vblk, ident), kv_spec(vblk, nxt),
            tab_spec(ident), tab_spec(ident), tab_spec(nxt), tab_spec(nxt),
            row_p, row_p,
            pl.BlockSpec(memory_space=pltpu.SMEM),
        ],
        out_specs=pl.BlockSpec((b, TILE, SWA_WIDTH), lambda i: (0, i, 0)),
        out_shape=jax.ShapeDtypeStruct((b, s, SWA_WIDTH), BF16),
        scratch_shapes=[pltpu.VMEM((3 * b, TILE, LANES), BF16)],
        compiler_params=pltpu.CompilerParams(
            dimension_semantics=("arbitrary",), vmem_limit_bytes=VMEM_LIMIT),
        name="swa",
    )(proj3d, proj3d, proj3d, proj3d, proj3d, proj3d,
      cos_t, sin_t, cos_t, sin_t, qw, kw, sinks)


def _out_kernel(of_ref, ob_ref, z_ref, sw_ref, swz_ref, x_ref, onw_ref, wo_ref, y_ref):
    dn = of_ref[...].astype(F32) + ob_ref[...].astype(F32)
    onw = onw_ref[...]
    parts = []
    for h in range(DN_HEADS):
        blk = dn[:, h * LANES:(h + 1) * LANES]
        ms = jnp.mean(blk * blk, axis=-1, keepdims=True)
        parts.append((blk * lax.rsqrt(ms + EPS)) * onw)
    dn_n = jnp.concatenate(parts, axis=1) * _silu(z_ref[...].astype(F32))
    sw = sw_ref[...].astype(F32) * _silu(swz_ref[...].astype(F32))
    mix = jnp.concatenate([dn_n, sw], axis=1).astype(BF16)
    y_ref[...] = x_ref[...] + jnp.dot(mix, wo_ref[...], preferred_element_type=F32)


def _out_projection(o_f, o_b, proj2d, sw, x2d, out_norm_w, w_out):
    rows = x2d.shape[0]
    half = lambda i: (i, 0)
    return pl.pallas_call(
        _out_kernel,
        grid=(rows // PROJ_ROWS,),
        in_specs=[
            pl.BlockSpec((PROJ_ROWS, DN_WIDTH), half),
            pl.BlockSpec((PROJ_ROWS, DN_WIDTH), half),
            pl.BlockSpec((PROJ_ROWS, DN_WIDTH), lambda i: (i, COL_DNZ // DN_WIDTH)),
            pl.BlockSpec((PROJ_ROWS, SWA_WIDTH), half),
            pl.BlockSpec((PROJ_ROWS, SWA_WIDTH), lambda i: (i, COL_SWZ // SWA_WIDTH)),
            pl.BlockSpec((PROJ_ROWS, D_MODEL), half),
            pl.BlockSpec((1, LANES), lambda i: (0, 0)),
            pl.BlockSpec((DN_WIDTH + SWA_WIDTH, D_MODEL), lambda i: (0, 0)),
        ],
        out_specs=pl.BlockSpec((PROJ_ROWS, D_MODEL), half),
        out_shape=jax.ShapeDtypeStruct((rows, D_MODEL), F32),
        compiler_params=pltpu.CompilerParams(
            dimension_semantics=("arbitrary",), vmem_limit_bytes=VMEM_LIMIT),
        name="out_proj",
    )(o_f, o_b, proj2d, sw, proj2d, x2d, out_norm_w, w_out)


def _rope_tables(seq):
    inv_freq = ROPE_THETA ** (-jnp.arange(0, SWA_HEAD_DIM, 2, dtype=F32) / SWA_HEAD_DIM)
    ang = jnp.arange(seq, dtype=F32)[:, None] * inv_freq[None, :]
    half = SWA_HEAD_DIM // 2
    lane = jnp.arange(LANES)
    expand = (lane[None, :] % half == jnp.arange(half)[:, None]).astype(F32)
    sign = jnp.where(lane % SWA_HEAD_DIM < half, -1.0, 1.0).astype(F32)
    cos = jnp.dot(jnp.cos(ang), expand, precision=lax.Precision.HIGHEST)
    sin = jnp.dot(jnp.sin(ang), expand, precision=lax.Precision.HIGHEST)
    return cos, sin * sign[None, :]


def _layer(x, norm_w, w_in, conv_w, a_log, dt_bias, out_norm_w, q_norm_w, k_norm_w, sinks, w_out):
    b, s, _ = x.shape
    x2d = x.reshape(b * s, D_MODEL)
    o_q, o_k, o_v, o_z = 0, DN_WIDTH, 2 * DN_WIDTH, 3 * DN_WIDTH
    o_beta = 4 * DN_WIDTH
    o_decay = o_beta + N_GATE
    o_swq = o_decay + N_GATE
    o_swk = o_swq + SWA_WIDTH
    o_swv = o_swk + SWA_KV_WIDTH
    o_swz = o_swv + SWA_KV_WIDTH
    w_gate = w_in[:, o_beta:o_swq]
    w_perm = jnp.concatenate([
        w_in[:, o_q:o_beta],
        w_in[:, o_swq:o_swk], w_in[:, o_swz:o_swz + SWA_WIDTH],
        w_in[:, o_swk:o_swv], w_in[:, o_swv:o_swz]], axis=1).astype(BF16)
    w_gate_pad = jnp.concatenate(
        [w_gate, jnp.zeros((D_MODEL, LANES - 2 * N_GATE), w_in.dtype)], axis=1).astype(BF16)
    conv_pad = jnp.concatenate([conv_w, jnp.zeros((SUBLANES - DN_CONV, 3 * DN_WIDTH), conv_w.dtype)], axis=0)
    proj2d, gate_c, gate_t, q, k, v = _in_projection_conv(
        x2d, s, norm_w.reshape(1, D_MODEL), w_perm, w_gate_pad, w_gate.T.astype(BF16), conv_pad)
    proj3d = proj2d.reshape(b, s, PROJ_WIDTH - 3 * DN_WIDTH)
    q, k, v = (a.reshape(b, s, DN_WIDTH) for a in (q, k, v))

    a_flat = a_log.reshape(N_GATE).astype(F32)
    d_flat = dt_bias.reshape(N_GATE).astype(F32)
    zeros8 = jnp.zeros((N_GATE,), F32)
    pad_row = jnp.zeros((LANES - 2 * N_GATE,), F32)
    al_row = jnp.concatenate([zeros8, a_flat, pad_row]).reshape(1, LANES)
    dt_row = jnp.concatenate([zeros8, d_flat, pad_row]).reshape(1, LANES)
    al_col = jnp.concatenate([zeros8, a_flat]).reshape(2 * N_GATE, 1)
    dt_col = jnp.concatenate([zeros8, d_flat]).reshape(2 * N_GATE, 1)
    o_f, o_b = _dn_scan(q, k, v, gate_c.reshape(b, s, LANES), gate_t, al_row, dt_row, al_col, dt_col)

    cos_t, sin_t = _rope_tables(s)
    reps = LANES // SWA_HEAD_DIM
    sw = _swa(proj3d, cos_t, sin_t,
              jnp.tile(q_norm_w.astype(F32), reps).reshape(1, LANES),
              jnp.tile(k_norm_w.astype(F32), reps).reshape(1, LANES),
              sinks.astype(F32))

    y = _out_projection(o_f.reshape(b * s, DN_WIDTH), o_b.reshape(b * s, DN_WIDTH), proj2d,
                        sw.reshape(b * s, SWA_WIDTH), x2d,
                        out_norm_w.reshape(1, LANES).astype(F32), w_out.astype(BF16))
    return y.reshape(b, s, D_MODEL)


def kernel(x, norm_w, w_in, dn_conv_w, dn_a_log, dn_dt_bias, dn_out_norm_w,
           swa_q_norm_w, swa_k_norm_w, swa_sinks, w_out):
    for l in range(norm_w.shape[0]):
        x = _layer(x, norm_w[l], w_in[l], dn_conv_w[l], dn_a_log[l], dn_dt_bias[l], dn_out_norm_w[l],
                   swa_q_norm_w[l], swa_k_norm_w[l], swa_sinks[l], w_out[l])
    return x
```

```python
import functools

import jax
import jax.numpy as jnp
from jax import lax
from jax.experimental import pallas as pl
from jax.experimental.pallas import tpu as pltpu

D_MODEL = 1024
DN_HEADS = 4
DN_HEAD_DIM = 128
DN_WIDTH = DN_HEADS * DN_HEAD_DIM
DN_CONV = 5
N_DIR = 2
N_GATE = N_DIR * DN_HEADS
SWA_Q_HEADS = 8
SWA_KV_HEADS = 2
SWA_HEAD_DIM = 64
SWA_WIDTH = SWA_Q_HEADS * SWA_HEAD_DIM
SWA_KV_WIDTH = SWA_KV_HEADS * SWA_HEAD_DIM
WINDOW = 128
ROPE_THETA = 10000.0
EPS = 1e-6

LANES = 128
SUBLANES = 8
TILE = 128
PROJ_ROWS = 512
OUT_ROWS = 1024
VMEM_LIMIT = 48 * 1024 * 1024

COL_DNZ = 0
COL_SWQ = COL_DNZ + DN_WIDTH
COL_SWZ = COL_SWQ + SWA_WIDTH
COL_SWK = COL_SWZ + SWA_WIDTH
COL_SWV = COL_SWK + SWA_KV_WIDTH
PROJ_WIDTH = 3 * DN_WIDTH + COL_SWV + SWA_KV_WIDTH
HALO = 16

F32 = jnp.float32
BF16 = jnp.bfloat16
NT_DIMS = (((1,), (1,)), ((), ()))
TN_DIMS = (((0,), (0,)), ((), ()))


def _silu(x):
    return x * jax.nn.sigmoid(x)


def _softplus(x):
    return jnp.maximum(x, 0.0) + jnp.log1p(jnp.exp(-jnp.abs(x)))


def _dot(a, b):
    return jnp.dot(a.astype(BF16), b.astype(BF16), preferred_element_type=F32)


def _dot_nt(a, b):
    return lax.dot_general(a.astype(BF16), b.astype(BF16), NT_DIMS, preferred_element_type=F32)


def _dot_tn(a, b):
    return lax.dot_general(a.astype(BF16), b.astype(BF16), TN_DIMS, preferred_element_type=F32)


def _proj_conv_kernel(tiles_per_seq, x_ref, nw_ref, w_ref, wg_ref, wgt_ref, cw_ref,
                      rest_ref, gc_ref, gt_ref, q_ref, k_ref, v_ref, buf_ref, new_ref):
    i = pl.program_id(0)
    n_tiles = pl.num_programs(0) - 1
    rows = x_ref.shape[0]
    pad = (DN_CONV - 1) // 2
    qkv_w = 3 * DN_WIDTH

    @pl.when(i == 0)
    def _():
        buf_ref[...] = jnp.zeros_like(buf_ref)

    x = x_ref[...]
    ms = jnp.mean(x * x, axis=-1, keepdims=True)
    h = ((x * lax.rsqrt(ms + EPS)) * nw_ref[...]).astype(BF16)
    boundary = (i % tiles_per_seq == 0) | (i == n_tiles)
    head = jnp.dot(h[0:HALO, :], w_ref[:, 0:qkv_w], preferred_element_type=F32)
    buf_ref[HALO + rows:2 * HALO + rows, :] = jnp.where(boundary, 0.0, head)
    new_ref[...] = jnp.dot(h, w_ref[:, 0:qkv_w], preferred_element_type=F32)

    gc_ref[...] = jnp.dot(h, wg_ref[...], preferred_element_type=F32)
    gt_ref[...] = lax.dot_general(wgt_ref[...], h, NT_DIMS, preferred_element_type=F32)

    outs = (q_ref, k_ref, v_ref)
    rest_w = rest_ref.shape[1]
    chunk = 2 * LANES
    n_chunks = rest_w // chunk
    n_blocks = 3 * DN_HEADS
    for c in range(n_blocks):
        for r in range(c * n_chunks // n_blocks, (c + 1) * n_chunks // n_blocks):
            cols = slice(r * chunk, (r + 1) * chunk)
            rest_ref[:, cols] = jnp.dot(h, w_ref[:, qkv_w + r * chunk:qkv_w + (r + 1) * chunk],
                                        preferred_element_type=F32).astype(BF16)
        lo = c * LANES
        window = buf_ref[:, lo:lo + LANES]
        acc = None
        for j in range(DN_CONV):
            shifted = window if j == pad else pltpu.roll(window, (pad - j) % window.shape[0], 0)
            term = shifted[HALO:HALO + rows] * cw_ref[j:j + 1, lo:lo + LANES]
            acc = term if acc is None else acc + term
        y = _silu(acc)
        part, head = divmod(c, DN_HEADS)
        if part < 2:
            y = y * lax.rsqrt(jnp.sum(y * y, axis=-1, keepdims=True) + EPS)
        if part == 0:
            y = y * (DN_HEAD_DIM ** -0.5)
        outs[part][:, head * LANES:(head + 1) * LANES] = y.astype(BF16)

    buf_ref[0:HALO, :] = jnp.where(boundary, 0.0, buf_ref[rows:rows + HALO, :])
    buf_ref[HALO:HALO + rows, :] = new_ref[...]


def _in_projection_conv(x2d, seq, norm_w, w_perm, w_gate, w_gate_t, conv_w):
    rows = x2d.shape[0]
    n_tiles = rows // PROJ_ROWS
    qkv_w = 3 * DN_WIDTH
    rest_w = PROJ_WIDTH - qkv_w
    clamp = lambda t: jnp.clip(t, 0, n_tiles - 1)
    cur = lambda i: (clamp(i), 0)
    cur_t = lambda i: (0, clamp(i))
    prev = lambda i: (clamp(i - 1), 0)
    const = lambda i: (0, 0)
    qkv_spec = pl.BlockSpec((PROJ_ROWS, DN_WIDTH), prev)
    qkv_sds = jax.ShapeDtypeStruct((rows, DN_WIDTH), BF16)
    return pl.pallas_call(
        functools.partial(_proj_conv_kernel, seq // PROJ_ROWS),
        grid=(n_tiles + 1,),
        in_specs=[
            pl.BlockSpec((PROJ_ROWS, D_MODEL), cur),
            pl.BlockSpec((1, D_MODEL), const),
            pl.BlockSpec((D_MODEL, PROJ_WIDTH), const),
            pl.BlockSpec((D_MODEL, LANES), const),
            pl.BlockSpec((2 * N_GATE, D_MODEL), const),
            pl.BlockSpec((SUBLANES, qkv_w), const),
        ],
        out_specs=[
            pl.BlockSpec((PROJ_ROWS, rest_w), cur),
            pl.BlockSpec((PROJ_ROWS, LANES), cur),
            pl.BlockSpec((2 * N_GATE, PROJ_ROWS), cur_t),
            qkv_spec, qkv_spec, qkv_spec,
        ],
        out_shape=[
            jax.ShapeDtypeStruct((rows, rest_w), BF16),
            jax.ShapeDtypeStruct((rows, LANES), F32),
            jax.ShapeDtypeStruct((2 * N_GATE, rows), F32),
            qkv_sds, qkv_sds, qkv_sds,
        ],
        scratch_shapes=[pltpu.VMEM((PROJ_ROWS + 2 * HALO, qkv_w), F32),
                        pltpu.VMEM((PROJ_ROWS, qkv_w), F32)],
        compiler_params=pltpu.CompilerParams(
            dimension_semantics=("arbitrary",), vmem_limit_bytes=VMEM_LIMIT),
        name="in_proj_conv",
    )(x2d, norm_w, w_perm, w_gate, w_gate_t, conv_w)


def _split3(x):
    x1 = x.astype(BF16)
    r1 = x - x1.astype(F32)
    x2 = r1.astype(BF16)
    x3 = (r1 - x2.astype(F32)).astype(BF16)
    return x1, x2, x3


def _exact_left(m_bf16, x):
    return sum(jnp.dot(m_bf16, p, preferred_element_type=F32) for p in _split3(x))


def _exact_right(x, m_bf16):
    return sum(jnp.dot(p, m_bf16, preferred_element_type=F32) for p in _split3(x))


def _tri_inverse_minus_eye(x0, row, col, lower):
    same = lambda s: (row // s) == (col // s)
    s = 2
    p = jnp.where(same(s), x0, 0.0)
    while s < TILE:
        m_off = jnp.where(same(2 * s) & jnp.logical_not(same(s)), -x0, 0.0)
        if s < SUBLANES:
            z = m_off + _dot(m_off, p)
            yield
            p = p - z - _dot(p, z)
            yield
        else:
            starts = range(s if lower else 0, TILE, 2 * s)
            take = lambda a: jnp.concatenate([a[r:r + s] for r in starts], axis=0)

            def put(rows, base):
                parts = [rows[(r // (2 * s)) * s:(r // (2 * s) + 1) * s] if r in starts else base[r:r + s]
                         for r in range(0, TILE, s)]
                return jnp.concatenate(parts, axis=0)

            m_rows = take(m_off)
            z_rows = m_rows + _dot(m_rows, p)
            yield
            new_rows = take(p) - z_rows - _dot(take(p), put(z_rows, jnp.zeros_like(p)))
            yield
            p = put(new_rows, p)
        s *= 2
    return p


def _dn_chain(q, k, v, beta, gcc, gcr, tot, incl, lower, diag, row, col, state_ref, o_ref, hs):
    decay = jnp.where(incl, jnp.exp(jnp.where(incl, gcc - gcr, 0.0)), 0.0)
    gram = _dot_nt(k, k)
    qk = _dot_nt(q, k)
    yield
    x0 = jnp.where(diag, 0.0, -(gram * decay * beta))
    p = yield from _tri_inverse_minus_eye(x0, row, col, lower)
    egc = jnp.exp(gcc)
    rhs = jnp.concatenate([v * beta, k * (beta * egc)], axis=1)
    uw = rhs + _dot(p, rhs)
    yield
    auw = _dot(qk * decay, uw)
    kt = _dot_tn(k * jnp.exp(tot - gcc), uw)
    yield
    p_q = q * egc - auw[:, TILE:]
    state = state_ref[...]
    res = _dot(jnp.concatenate([kt[:, TILE:], p_q], axis=0), state)
    yield
    o_ref[:, hs] = (res[TILE:] + auw[:, :TILE]).astype(BF16)
    state_ref[...] = jnp.exp(tot) * state - res[:TILE] + kt[:, :TILE]


def _dn_kernel(batch, *refs):
    n_in = 4 + batch
    fwd_refs, bwd_refs = refs[0:n_in], refs[n_in:2 * n_in]
    alr_ref, dtr_ref, alc_ref, dtc_ref, of_ref, ob_ref, s_ref = refs[2 * n_in:]

    @pl.when(pl.program_id(0) == 0)
    def _():
        s_ref[...] = jnp.zeros_like(s_ref)

    row = lax.broadcasted_iota(jnp.int32, (TILE, TILE), 0)
    col = lax.broadcasted_iota(jnp.int32, (TILE, TILE), 1)
    lower = (col <= row)
    upper = (col >= row)
    diag = (col == row)
    lower_bf = lower.astype(BF16)
    upper_bf = upper.astype(BF16)
    neg_a_row = -jnp.exp(alr_ref[...])
    neg_a_col = -jnp.exp(alc_ref[...])

    streams = ((fwd_refs, of_ref, lower, lower_bf, upper_bf, TILE - 1),
               (bwd_refs, ob_ref, upper, upper_bf, lower_bf, 0))
    chains = []
    for d, (in_refs, o_ref, incl, cum_left, cum_right, last) in enumerate(streams):
        q_ref, k_ref, v_ref, gc_ref = in_refs[0:4]
        for b in range(batch):
            gate_c = gc_ref[b]
            gate_r = in_refs[4 + b][...]
            beta_c = jax.nn.sigmoid(gate_c)
            g_c = neg_a_row * _softplus(gate_c + dtr_ref[...])
            g_r = neg_a_col * _softplus(gate_r + dtc_ref[...])
            cum_c = _exact_left(cum_left, g_c)
            tot_c = cum_c[last:last + 1, :]
            cum_r = _exact_right(g_r, cum_right)
            for h in range(DN_HEADS):
                cb = d * DN_HEADS + h
                cg = N_GATE + cb
                hs = slice(h * LANES, (h + 1) * LANES)
                chains.append(_dn_chain(
                    q_ref[b, :, hs].astype(F32), k_ref[b, :, hs].astype(F32), v_ref[b, :, hs].astype(F32),
                    beta_c[:, cb:cb + 1], cum_c[:, cg:cg + 1], cum_r[cg:cg + 1, :], tot_c[:, cg:cg + 1],
                    incl, d == 0, diag, row, col, s_ref.at[b * N_GATE + cb], o_ref.at[b], hs))
    while chains:
        alive = []
        for chain in chains:
            try:
                next(chain)
                alive.append(chain)
            except StopIteration:
                pass
        chains = alive


def _dn_scan(q, k, v, gate_c, gate_t, al_row, dt_row, al_col, dt_col):
    b, s, _ = q.shape
    n_tiles = s // TILE
    fwd = lambda i: (0, i, 0)
    bwd = lambda i: (0, n_tiles - 1 - i, 0)
    const = lambda i: (0, 0)
    qkv_f = pl.BlockSpec((b, TILE, DN_WIDTH), fwd)
    qkv_b = pl.BlockSpec((b, TILE, DN_WIDTH), bwd)
    gc_f = pl.BlockSpec((b, TILE, LANES), fwd)
    gc_b = pl.BlockSpec((b, TILE, LANES), bwd)
    gr_f = [pl.BlockSpec((2 * N_GATE, TILE), functools.partial(lambda bb, i: (0, bb * n_tiles + i), bb))
            for bb in range(b)]
    gr_b = [pl.BlockSpec((2 * N_GATE, TILE),
                         functools.partial(lambda bb, i: (0, bb * n_tiles + n_tiles - 1 - i), bb))
            for bb in range(b)]
    row_p = pl.BlockSpec((1, LANES), const)
    col_p = pl.BlockSpec((2 * N_GATE, 1), const)
    out_sds = jax.ShapeDtypeStruct((b, s, DN_WIDTH), BF16)
    return pl.pallas_call(
        functools.partial(_dn_kernel, b),
        grid=(n_tiles,),
        in_specs=[qkv_f, qkv_f, qkv_f, gc_f, *gr_f,
                  qkv_b, qkv_b, qkv_b, gc_b, *gr_b,
                  row_p, row_p, col_p, col_p],
        out_specs=[qkv_f, qkv_b],
        out_shape=[out_sds, out_sds],
        scratch_shapes=[pltpu.VMEM((b * N_GATE, DN_HEAD_DIM, DN_HEAD_DIM), F32)],
        compiler_params=pltpu.CompilerParams(
            dimension_semantics=("arbitrary",), vmem_limit_bytes=VMEM_LIMIT),
        name="dn_scan",
    )(q, k, v, gate_c, *([gate_t] * b), q, k, v, gate_c, *([gate_t] * b), al_row, dt_row, al_col, dt_col)


def _pair_rms(x, w, head_ones):
    x2 = x * x
    hi = x2.astype(BF16)
    lo = (x2 - hi.astype(F32)).astype(BF16)
    ss = (jnp.dot(hi, head_ones, preferred_element_type=F32)
          + jnp.dot(lo, head_ones, preferred_element_type=F32))
    return (x * lax.rsqrt(ss * (1.0 / SWA_HEAD_DIM) + EPS)) * w


def _pair_rope(x, cos, sin_signed, first_quarter):
    half = SWA_HEAD_DIM // 2
    rot = jnp.where(first_quarter, pltpu.roll(x, LANES - half, 1), pltpu.roll(x, half, 1))
    return x * cos + rot * sin_signed


def _swa_head(q_m, k_span, v_span, valid, sink, o_ref, lanes, needs_roll):
    sc = lax.dot_general(q_m, k_span, NT_DIMS, preferred_element_type=F32)
    yield
    sc = jnp.where(valid, sc, -1e30)
    mx = jnp.maximum(jnp.max(sc, axis=-1, keepdims=True), sink)
    yield
    p = jnp.exp(sc - mx)
    denom = jnp.sum(p, axis=-1, keepdims=True) + jnp.exp(sink - mx)
    yield
    pv = jnp.dot(p.astype(BF16), v_span, preferred_element_type=F32) / denom
    if needs_roll:
        pv = pltpu.roll(pv, SWA_HEAD_DIM, 1)
    o_ref[:, lanes] = pv[:, lanes.start % LANES:(lanes.start % LANES) + SWA_HEAD_DIM].astype(BF16)


def _swa_kernel(q_ref, kc_ref, kn_ref, vp_ref, vc_ref, vn_ref,
                cc_ref, sc_ref, cn_ref, sn_ref,
                qw_ref, kw_ref, sink_ref, o_ref, ks_ref):
    i = pl.program_id(0)
    n = pl.num_programs(0)
    batch = q_ref.shape[0]
    lane = lax.broadcasted_iota(jnp.int32, (TILE, LANES), 1)
    lo_half = lane < SWA_HEAD_DIM
    first_quarter = (lane % SWA_HEAD_DIM) < (SWA_HEAD_DIM // 2)

    sub = lax.broadcasted_iota(jnp.int32, (LANES, LANES), 0)
    head_ones = ((sub // SWA_HEAD_DIM) == (lane // SWA_HEAD_DIM)).astype(BF16)

    def prep(x, w, cos_ref, sin_ref):
        return _pair_rope(_pair_rms(x, w, head_ones), cos_ref[...], sin_ref[...], first_quarter)

    kw = kw_ref[...]

    @pl.when(i == 0)
    def _():
        for b in range(batch):
            ks_ref[3 * b] = jnp.zeros((TILE, LANES), BF16)
            ks_ref[3 * b + 1] = prep(kc_ref[b].astype(F32), kw, cc_ref, sc_ref).astype(BF16)

    @pl.when(i > 0)
    def _():
        for b in range(batch):
            ks_ref[3 * b] = ks_ref[3 * b + 1]
            ks_ref[3 * b + 1] = ks_ref[3 * b + 2]

    for b in range(batch):
        ks_ref[3 * b + 2] = prep(kn_ref[b].astype(F32), kw, cn_ref, sn_ref).astype(BF16)
    span = 3 * TILE
    r = lax.broadcasted_iota(jnp.int32, (TILE, span), 0)
    c = lax.broadcasted_iota(jnp.int32, (TILE, span), 1)
    valid = (jnp.abs(c - WINDOW - r) <= WINDOW)
    valid = valid & ((c >= TILE) | (i > 0)) & ((c < 2 * TILE) | (i < n - 1))
    group = SWA_Q_HEADS // SWA_KV_HEADS
    qw = qw_ref[...]
    scale = SWA_HEAD_DIM ** -0.5
    heads = []
    for b in range(batch):
        k_span = ks_ref[3 * b:3 * b + 3].reshape(3 * TILE, LANES)
        v_span = jnp.concatenate([vp_ref[b], vc_ref[b], vn_ref[b]], axis=0)
        for j in range(SWA_Q_HEADS // 2):
            q_pair = prep(q_ref[b, :, j * LANES:(j + 1) * LANES].astype(F32), qw, cc_ref, sc_ref) * scale
            q_swap = pltpu.roll(q_pair, SWA_HEAD_DIM, 1)
            for half in range(2):
                hq = 2 * j + half
                g = hq // group
                q_al = q_pair if half == g else q_swap
                q_m = jnp.where(lo_half if g == 0 else jnp.logical_not(lo_half), q_al, 0.0).astype(BF16)
                lanes = slice(hq * SWA_HEAD_DIM, (hq + 1) * SWA_HEAD_DIM)
                heads.append(_swa_head(q_m, k_span, v_span, valid, sink_ref[hq], o_ref.at[b], lanes,
                                       half != g))
    while heads:
        alive = []
        for head in heads:
            try:
                next(head)
                alive.append(head)
            except StopIteration:
                pass
        heads = alive


def _swa(proj3d, cos_t, sin_t, qw, kw, sinks):
    b, s, _ = proj3d.shape
    n_tiles = s // TILE
    kblk = COL_SWK // LANES
    vblk = COL_SWV // LANES
    prev = lambda i: jnp.maximum(i - 1, 0)
    nxt = lambda i: jnp.minimum(i + 1, n_tiles - 1)

    def kv_spec(blk, f):
        return pl.BlockSpec((b, TILE, LANES), lambda i: (0, f(i), blk))

    def tab_spec(f):
        return pl.BlockSpec((TILE, LANES), lambda i: (f(i), 0))

    ident = lambda i: i
    row_p = pl.BlockSpec((1, LANES), lambda i: (0, 0))
    return pl.pallas_call(
        _swa_kernel,
        grid=(n_tiles,),
        in_specs=[
            pl.BlockSpec((b, TILE, SWA_WIDTH), lambda i: (0, i, COL_SWQ // SWA_WIDTH)),
            kv_spec(kblk, ident), kv_spec(kblk, nxt),
            kv_spec(vblk, prev), kv_spec(vblk, ident), kv_spec(vblk, nxt),
            tab_spec(ident), tab_spec(ident), tab_spec(nxt), tab_spec(nxt),
            row_p, row_p,
            pl.BlockSpec(memory_space=pltpu.SMEM),
        ],
        out_specs=pl.BlockSpec((b, TILE, SWA_WIDTH), lambda i: (0, i, 0)),
        out_shape=jax.ShapeDtypeStruct((b, s, SWA_WIDTH), BF16),
        scratch_shapes=[pltpu.VMEM((3 * b, TILE, LANES), BF16)],
        compiler_params=pltpu.CompilerParams(
            dimension_semantics=("arbitrary",), vmem_limit_bytes=VMEM_LIMIT),
        name="swa",
    )(proj3d, proj3d, proj3d, proj3d, proj3d, proj3d,
      cos_t, sin_t, cos_t, sin_t, qw, kw, sinks)


def _out_kernel(of_ref, ob_ref, z_ref, sw_ref, swz_ref, x_ref, onw_ref, wo_ref, y_ref):
    dn = of_ref[...].astype(F32) + ob_ref[...].astype(F32)
    onw = onw_ref[...]
    parts = []
    for h in range(DN_HEADS):
        blk = dn[:, h * LANES:(h + 1) * LANES]
        ms = jnp.mean(blk * blk, axis=-1, keepdims=True)
        parts.append((blk * lax.rsqrt(ms + EPS)) * onw)
    dn_n = jnp.concatenate(parts, axis=1) * _silu(z_ref[...].astype(F32))
    sw = sw_ref[...].astype(F32) * _silu(swz_ref[...].astype(F32))
    mix = jnp.concatenate([dn_n, sw], axis=1).astype(BF16)
    y_ref[...] = x_ref[...] + jnp.dot(mix, wo_ref[...], preferred_element_type=F32)


def _out_projection(o_f, o_b, proj2d, sw, x2d, out_norm_w, w_out):
    rows = x2d.shape[0]
    half = lambda i: (i, 0)
    return pl.pallas_call(
        _out_kernel,
        grid=(rows // OUT_ROWS,),
        in_specs=[
            pl.BlockSpec((OUT_ROWS, DN_WIDTH), half),
            pl.BlockSpec((OUT_ROWS, DN_WIDTH), half),
            pl.BlockSpec((OUT_ROWS, DN_WIDTH), lambda i: (i, COL_DNZ // DN_WIDTH)),
            pl.BlockSpec((OUT_ROWS, SWA_WIDTH), half),
            pl.BlockSpec((OUT_ROWS, SWA_WIDTH), lambda i: (i, COL_SWZ // SWA_WIDTH)),
            pl.BlockSpec((OUT_ROWS, D_MODEL), half),
            pl.BlockSpec((1, LANES), lambda i: (0, 0)),
            pl.BlockSpec((DN_WIDTH + SWA_WIDTH, D_MODEL), lambda i: (0, 0)),
        ],
        out_specs=pl.BlockSpec((OUT_ROWS, D_MODEL), half),
        out_shape=jax.ShapeDtypeStruct((rows, D_MODEL), F32),
        compiler_params=pltpu.CompilerParams(
            dimension_semantics=("arbitrary",), vmem_limit_bytes=VMEM_LIMIT),
        name="out_proj",
    )(o_f, o_b, proj2d, sw, proj2d, x2d, out_norm_w, w_out)


def _rope_tables(seq):
    inv_freq = ROPE_THETA ** (-jnp.arange(0, SWA_HEAD_DIM, 2, dtype=F32) / SWA_HEAD_DIM)
    ang = jnp.arange(seq, dtype=F32)[:, None] * inv_freq[None, :]
    half = SWA_HEAD_DIM // 2
    lane = jnp.arange(LANES)
    expand = (lane[None, :] % half == jnp.arange(half)[:, None]).astype(F32)
    sign = jnp.where(lane % SWA_HEAD_DIM < half, -1.0, 1.0).astype(F32)
    cos = jnp.dot(jnp.cos(ang), expand, precision=lax.Precision.HIGHEST)
    sin = jnp.dot(jnp.sin(ang), expand, precision=lax.Precision.HIGHEST)
    return cos, sin * sign[None, :]


def _layer(x, norm_w, w_in, conv_w, a_log, dt_bias, out_norm_w, q_norm_w, k_norm_w, sinks, w_out):
    b, s, _ = x.shape
    x2d = x.reshape(b * s, D_MODEL)
    o_q, o_k, o_v, o_z = 0, DN_WIDTH, 2 * DN_WIDTH, 3 * DN_WIDTH
    o_beta = 4 * DN_WIDTH
    o_decay = o_beta + N_GATE
    o_swq = o_decay + N_GATE
    o_swk = o_swq + SWA_WIDTH
    o_swv = o_swk + SWA_KV_WIDTH
    o_swz = o_swv + SWA_KV_WIDTH
    w_gate = w_in[:, o_beta:o_swq]
    w_perm = jnp.concatenate([
        w_in[:, o_q:o_beta],
        w_in[:, o_swq:o_swk], w_in[:, o_swz:o_swz + SWA_WIDTH],
        w_in[:, o_swk:o_swv], w_in[:, o_swv:o_swz]], axis=1).astype(BF16)
    w_gate_pad = jnp.concatenate(
        [w_gate, jnp.zeros((D_MODEL, LANES - 2 * N_GATE), w_in.dtype)], axis=1).astype(BF16)
    conv_pad = jnp.concatenate([conv_w, jnp.zeros((SUBLANES - DN_CONV, 3 * DN_WIDTH), conv_w.dtype)], axis=0)
    proj2d, gate_c, gate_t, q, k, v = _in_projection_conv(
        x2d, s, norm_w.reshape(1, D_MODEL), w_perm, w_gate_pad, w_gate.T.astype(BF16), conv_pad)
    proj3d = proj2d.reshape(b, s, PROJ_WIDTH - 3 * DN_WIDTH)
    q, k, v = (a.reshape(b, s, DN_WIDTH) for a in (q, k, v))

    a_flat = a_log.reshape(N_GATE).astype(F32)
    d_flat = dt_bias.reshape(N_GATE).astype(F32)
    zeros8 = jnp.zeros((N_GATE,), F32)
    pad_row = jnp.zeros((LANES - 2 * N_GATE,), F32)
    al_row = jnp.concatenate([zeros8, a_flat, pad_row]).reshape(1, LANES)
    dt_row = jnp.concatenate([zeros8, d_flat, pad_row]).reshape(1, LANES)
    al_col = jnp.concatenate([zeros8, a_flat]).reshape(2 * N_GATE, 1)
    dt_col = jnp.concatenate([zeros8, d_flat]).reshape(2 * N_GATE, 1)
    o_f, o_b = _dn_scan(q, k, v, gate_c.reshape(b, s, LANES), gate_t, al_row, dt_row, al_col, dt_col)

    cos_t, sin_t = _rope_tables(s)
    reps = LANES // SWA_HEAD_DIM
    sw = _swa(proj3d, cos_t, sin_t,
              jnp.tile(q_norm_w.astype(F32), reps).reshape(1, LANES),
              jnp.tile(k_norm_w.astype(F32), reps).reshape(1, LANES),
              sinks.astype(F32))

    y = _out_projection(o_f.reshape(b * s, DN_WIDTH), o_b.reshape(b * s, DN_WIDTH), proj2d,
                        sw.reshape(b * s, SWA_WIDTH), x2d,
                        out_norm_w.reshape(1, LANES).astype(F32), w_out.astype(BF16))
    return y.reshape(b, s, D_MODEL)


def kernel(x, norm_w, w_in, dn_conv_w, dn_a_log, dn_dt_bias, dn_out_norm_w,
           swa_q_norm_w, swa_k_norm_w, swa_sinks, w_out):
    for l in range(norm_w.shape[0]):
        x = _layer(x, norm_w[l], w_in[l], dn_conv_w[l], dn_a_log[l], dn_dt_bias[l], dn_out_norm_w[l],
                   swa_q_norm_w[l], swa_k_norm_w[l], swa_sinks[l], w_out[l])
    return x
```

```python
import functools

import jax
import jax.numpy as jnp
from jax import lax
from jax.experimental import pallas as pl
from jax.experimental.pallas import tpu as pltpu

D_MODEL = 1024
DN_HEADS = 4
DN_HEAD_DIM = 128
DN_WIDTH = DN_HEADS * DN_HEAD_DIM
DN_CONV = 5
N_DIR = 2
N_GATE = N_DIR * DN_HEADS
SWA_Q_HEADS = 8
SWA_KV_HEADS = 2
SWA_HEAD_DIM = 64
SWA_WIDTH = SWA_Q_HEADS * SWA_HEAD_DIM
SWA_KV_WIDTH = SWA_KV_HEADS * SWA_HEAD_DIM
WINDOW = 128
ROPE_THETA = 10000.0
EPS = 1e-6

LANES = 128
SUBLANES = 8
TILE = 128
PROJ_ROWS = 512
OUT_ROWS = 1024
VMEM_LIMIT = 48 * 1024 * 1024

COL_DNZ = 0
COL_SWQ = COL_DNZ + DN_WIDTH
COL_SWZ = COL_SWQ + SWA_WIDTH
COL_SWK = COL_SWZ + SWA_WIDTH
COL_SWV = COL_SWK + SWA_KV_WIDTH
PROJ_WIDTH = 3 * DN_WIDTH + COL_SWV + SWA_KV_WIDTH
HALO = 16

F32 = jnp.float32
BF16 = jnp.bfloat16
NT_DIMS = (((1,), (1,)), ((), ()))
TN_DIMS = (((0,), (0,)), ((), ()))


def _silu(x):
    return x * jax.nn.sigmoid(x)


def _softplus(x):
    return jnp.maximum(x, 0.0) + jnp.log1p(jnp.exp(-jnp.abs(x)))


def _dot(a, b):
    return jnp.dot(a.astype(BF16), b.astype(BF16), preferred_element_type=F32)


def _dot_nt(a, b):
    return lax.dot_general(a.astype(BF16), b.astype(BF16), NT_DIMS, preferred_element_type=F32)


def _dot_tn(a, b):
    return lax.dot_general(a.astype(BF16), b.astype(BF16), TN_DIMS, preferred_element_type=F32)


def _proj_conv_kernel(tiles_per_seq, x_ref, nw_ref, w_ref, wg_ref, wgt_ref, cw_ref,
                      rest_ref, gc_ref, gt_ref, q_ref, k_ref, v_ref, buf_ref, new_ref):
    i = pl.program_id(0)
    n_tiles = pl.num_programs(0) - 1
    rows = x_ref.shape[0]
    pad = (DN_CONV - 1) // 2
    qkv_w = 3 * DN_WIDTH

    @pl.when(i == 0)
    def _():
        buf_ref[...] = jnp.zeros_like(buf_ref)

    x = x_ref[...]
    ms = jnp.mean(x * x, axis=-1, keepdims=True)
    h = ((x * lax.rsqrt(ms + EPS)) * nw_ref[...]).astype(BF16)
    boundary = (i % tiles_per_seq == 0) | (i == n_tiles)
    new_ref[...] = jnp.dot(h, w_ref[:, 0:qkv_w], preferred_element_type=F32)
    buf_ref[HALO + rows:2 * HALO + rows, :] = jnp.where(boundary, 0.0, new_ref[0:HALO, :])

    gc_ref[...] = jnp.dot(h, wg_ref[...], preferred_element_type=F32)
    gt_ref[...] = lax.dot_general(wgt_ref[...], h, NT_DIMS, preferred_element_type=F32)

    outs = (q_ref, k_ref, v_ref)
    rest_w = rest_ref.shape[1]
    chunk = 2 * LANES
    n_chunks = rest_w // chunk
    n_blocks = 3 * DN_HEADS
    for c in range(n_blocks):
        for r in range(c * n_chunks // n_blocks, (c + 1) * n_chunks // n_blocks):
            cols = slice(r * chunk, (r + 1) * chunk)
            rest_ref[:, cols] = jnp.dot(h, w_ref[:, qkv_w + r * chunk:qkv_w + (r + 1) * chunk],
                                        preferred_element_type=F32).astype(BF16)
        lo = c * LANES
        window = buf_ref[:, lo:lo + LANES]
        acc = None
        for j in range(DN_CONV):
            shifted = window if j == pad else pltpu.roll(window, (pad - j) % window.shape[0], 0)
            term = shifted[HALO:HALO + rows] * cw_ref[j:j + 1, lo:lo + LANES]
            acc = term if acc is None else acc + term
        y = _silu(acc)
        part, head = divmod(c, DN_HEADS)
        if part < 2:
            y = y * lax.rsqrt(jnp.sum(y * y, axis=-1, keepdims=True) + EPS)
        if part == 0:
            y = y * (DN_HEAD_DIM ** -0.5)
        outs[part][:, head * LANES:(head + 1) * LANES] = y.astype(BF16)

    buf_ref[0:HALO, :] = jnp.where(boundary, 0.0, buf_ref[rows:rows + HALO, :])
    buf_ref[HALO:HALO + rows, :] = new_ref[...]


def _in_projection_conv(x2d, seq, norm_w, w_perm, w_gate, w_gate_t, conv_w):
    rows = x2d.shape[0]
    n_tiles = rows // PROJ_ROWS
    qkv_w = 3 * DN_WIDTH
    rest_w = PROJ_WIDTH - qkv_w
    clamp = lambda t: jnp.clip(t, 0, n_tiles - 1)
    cur = lambda i: (clamp(i), 0)
    cur_t = lambda i: (0, clamp(i))
    prev = lambda i: (clamp(i - 1), 0)
    const = lambda i: (0, 0)
    qkv_spec = pl.BlockSpec((PROJ_ROWS, DN_WIDTH), prev)
    qkv_sds = jax.ShapeDtypeStruct((rows, DN_WIDTH), BF16)
    return pl.pallas_call(
        functools.partial(_proj_conv_kernel, seq // PROJ_ROWS),
        grid=(n_tiles + 1,),
        in_specs=[
            pl.BlockSpec((PROJ_ROWS, D_MODEL), cur),
            pl.BlockSpec((1, D_MODEL), const),
            pl.BlockSpec((D_MODEL, PROJ_WIDTH), const),
            pl.BlockSpec((D_MODEL, LANES), const),
            pl.BlockSpec((2 * N_GATE, D_MODEL), const),
            pl.BlockSpec((SUBLANES, qkv_w), const),
        ],
        out_specs=[
            pl.BlockSpec((PROJ_ROWS, rest_w), cur),
            pl.BlockSpec((PROJ_ROWS, LANES), cur),
            pl.BlockSpec((2 * N_GATE, PROJ_ROWS), cur_t),
            qkv_spec, qkv_spec, qkv_spec,
        ],
        out_shape=[
            jax.ShapeDtypeStruct((rows, rest_w), BF16),
            jax.ShapeDtypeStruct((rows, LANES), F32),
            jax.ShapeDtypeStruct((2 * N_GATE, rows), F32),
            qkv_sds, qkv_sds, qkv_sds,
        ],
        scratch_shapes=[pltpu.VMEM((PROJ_ROWS + 2 * HALO, qkv_w), F32),
                        pltpu.VMEM((PROJ_ROWS, qkv_w), F32)],
        compiler_params=pltpu.CompilerParams(
            dimension_semantics=("arbitrary",), vmem_limit_bytes=VMEM_LIMIT),
        name="in_proj_conv",
    )(x2d, norm_w, w_perm, w_gate, w_gate_t, conv_w)


def _split3(x):
    x1 = x.astype(BF16)
    r1 = x - x1.astype(F32)
    x2 = r1.astype(BF16)
    x3 = (r1 - x2.astype(F32)).astype(BF16)
    return x1, x2, x3


def _exact_left(m_bf16, x):
    return sum(jnp.dot(m_bf16, p, preferred_element_type=F32) for p in _split3(x))


def _exact_right(x, m_bf16):
    return sum(jnp.dot(p, m_bf16, preferred_element_type=F32) for p in _split3(x))


def _tri_inverse_minus_eye(x0, row, col, lower):
    same = lambda s: (row // s) == (col // s)
    s = 2
    p = jnp.where(same(s), x0, 0.0)
    while s < TILE:
        m_off = jnp.where(same(2 * s) & jnp.logical_not(same(s)), -x0, 0.0)
        if s < SUBLANES:
            z = m_off + _dot(m_off, p)
            yield
            p = p - z - _dot(p, z)
            yield
        else:
            starts = range(s if lower else 0, TILE, 2 * s)
            take = lambda a: jnp.concatenate([a[r:r + s] for r in starts], axis=0)

            def put(rows, base):
                parts = [rows[(r // (2 * s)) * s:(r // (2 * s) + 1) * s] if r in starts else base[r:r + s]
                         for r in range(0, TILE, s)]
                return jnp.concatenate(parts, axis=0)

            m_rows = take(m_off)
            z_rows = m_rows + _dot(m_rows, p)
            yield
            new_rows = take(p) - z_rows - _dot(take(p), put(z_rows, jnp.zeros_like(p)))
            yield
            p = put(new_rows, p)
        s *= 2
    return p


def _dn_chain(q, k, v, beta, gcc, gcr, tot, incl, lower, diag, row, col, state_ref, o_ref, hs):
    decay = jnp.where(incl, jnp.exp(jnp.where(incl, gcc - gcr, 0.0)), 0.0)
    gram = _dot_nt(k, k)
    qk = _dot_nt(q, k)
    yield
    x0 = jnp.where(diag, 0.0, -(gram * decay * beta))
    p = yield from _tri_inverse_minus_eye(x0, row, col, lower)
    egc = jnp.exp(gcc)
    rhs = jnp.concatenate([v * beta, k * (beta * egc)], axis=1)
    uw = rhs + _dot(p, rhs)
    yield
    auw = _dot(qk * decay, uw)
    kt = _dot_tn(k * jnp.exp(tot - gcc), uw)
    yield
    p_q = q * egc - auw[:, TILE:]
    state = state_ref[...]
    res = _dot(jnp.concatenate([kt[:, TILE:], p_q], axis=0), state)
    yield
    o_ref[:, hs] = (res[TILE:] + auw[:, :TILE]).astype(BF16)
    state_ref[...] = jnp.exp(tot) * state - res[:TILE] + kt[:, :TILE]


def _dn_kernel(batch, *refs):
    n_in = 4 + batch
    fwd_refs, bwd_refs = refs[0:n_in], refs[n_in:2 * n_in]
    alr_ref, dtr_ref, alc_ref, dtc_ref, of_ref, ob_ref, s_ref = refs[2 * n_in:]

    @pl.when(pl.program_id(0) == 0)
    def _():
        s_ref[...] = jnp.zeros_like(s_ref)

    row = lax.broadcasted_iota(jnp.int32, (TILE, TILE), 0)
    col = lax.broadcasted_iota(jnp.int32, (TILE, TILE), 1)
    lower = (col <= row)
    upper = (col >= row)
    diag = (col == row)
    lower_bf = lower.astype(BF16)
    upper_bf = upper.astype(BF16)
    neg_a_row = -jnp.exp(alr_ref[...])
    neg_a_col = -jnp.exp(alc_ref[...])

    streams = ((fwd_refs, of_ref, lower, lower_bf, upper_bf, TILE - 1),
               (bwd_refs, ob_ref, upper, upper_bf, lower_bf, 0))
    chains = []
    for d, (in_refs, o_ref, incl, cum_left, cum_right, last) in enumerate(streams):
        q_ref, k_ref, v_ref, gc_ref = in_refs[0:4]
        for b in range(batch):
            gate_c = gc_ref[b]
            gate_r = in_refs[4 + b][...]
            beta_c = jax.nn.sigmoid(gate_c)
            g_c = neg_a_row * _softplus(gate_c + dtr_ref[...])
            g_r = neg_a_col * _softplus(gate_r + dtc_ref[...])
            cum_c = _exact_left(cum_left, g_c)
            tot_c = cum_c[last:last + 1, :]
            cum_r = _exact_right(g_r, cum_right)
            for h in range(DN_HEADS):
                cb = d * DN_HEADS + h
                cg = N_GATE + cb
                hs = slice(h * LANES, (h + 1) * LANES)
                chains.append(_dn_chain(
                    q_ref[b, :, hs].astype(F32), k_ref[b, :, hs].astype(F32), v_ref[b, :, hs].astype(F32),
                    beta_c[:, cb:cb + 1], cum_c[:, cg:cg + 1], cum_r[cg:cg + 1, :], tot_c[:, cg:cg + 1],
                    incl, d == 0, diag, row, col, s_ref.at[b * N_GATE + cb], o_ref.at[b], hs))
    while chains:
        alive = []
        for chain in chains:
            try:
                next(chain)
                alive.append(chain)
            except StopIteration:
                pass
        chains = alive


def _dn_scan(q, k, v, gate_c, gate_t, al_row, dt_row, al_col, dt_col):
    b, s, _ = q.shape
    n_tiles = s // TILE
    fwd = lambda i: (0, i, 0)
    bwd = lambda i: (0, n_tiles - 1 - i, 0)
    const = lambda i: (0, 0)
    qkv_f = pl.BlockSpec((b, TILE, DN_WIDTH), fwd)
    qkv_b = pl.BlockSpec((b, TILE, DN_WIDTH), bwd)
    gc_f = pl.BlockSpec((b, TILE, LANES), fwd)
    gc_b = pl.BlockSpec((b, TILE, LANES), bwd)
    gr_f = [pl.BlockSpec((2 * N_GATE, TILE), functools.partial(lambda bb, i: (0, bb * n_tiles + i), bb))
            for bb in range(b)]
    gr_b = [pl.BlockSpec((2 * N_GATE, TILE),
                         functools.partial(lambda bb, i: (0, bb * n_tiles + n_tiles - 1 - i), bb))
            for bb in range(b)]
    row_p = pl.BlockSpec((1, LANES), const)
    col_p = pl.BlockSpec((2 * N_GATE, 1), const)
    out_sds = jax.ShapeDtypeStruct((b, s, DN_WIDTH), BF16)
    return pl.pallas_call(
        functools.partial(_dn_kernel, b),
        grid=(n_tiles,),
        in_specs=[qkv_f, qkv_f, qkv_f, gc_f, *gr_f,
                  qkv_b, qkv_b, qkv_b, gc_b, *gr_b,
                  row_p, row_p, col_p, col_p],
        out_specs=[qkv_f, qkv_b],
        out_shape=[out_sds, out_sds],
        scratch_shapes=[pltpu.VMEM((b * N_GATE, DN_HEAD_DIM, DN_HEAD_DIM), F32)],
        compiler_params=pltpu.CompilerParams(
            dimension_semantics=("arbitrary",), vmem_limit_bytes=VMEM_LIMIT),
        name="dn_scan",
    )(q, k, v, gate_c, *([gate_t] * b), q, k, v, gate_c, *([gate_t] * b), al_row, dt_row, al_col, dt_col)


def _pair_rms(x, w, head_ones):
    x2 = x * x
    hi = x2.astype(BF16)
    lo = (x2 - hi.astype(F32)).astype(BF16)
    ss = (jnp.dot(hi, head_ones, preferred_element_type=F32)
          + jnp.dot(lo, head_ones, preferred_element_type=F32))
    return (x * lax.rsqrt(ss * (1.0 / SWA_HEAD_DIM) + EPS)) * w


def _pair_rope(x, cos, sin_signed, first_quarter):
    half = SWA_HEAD_DIM // 2
    rot = jnp.where(first_quarter, pltpu.roll(x, LANES - half, 1), pltpu.roll(x, half, 1))
    return x * cos + rot * sin_signed


def _swa_head(q_m, k_span, v_span, valid, sink, o_ref, lanes, needs_roll):
    sc = lax.dot_general(q_m, k_span, NT_DIMS, preferred_element_type=F32)
    yield
    sc = jnp.where(valid, sc, -1e30)
    mx = jnp.maximum(jnp.max(sc, axis=-1, keepdims=True), sink)
    yield
    p = jnp.exp(sc - mx)
    denom = jnp.sum(p, axis=-1, keepdims=True) + jnp.exp(sink - mx)
    yield
    pv = jnp.dot(p.astype(BF16), v_span, preferred_element_type=F32) / denom
    if needs_roll:
        pv = pltpu.roll(pv, SWA_HEAD_DIM, 1)
    o_ref[:, lanes] = pv[:, lanes.start % LANES:(lanes.start % LANES) + SWA_HEAD_DIM].astype(BF16)


def _swa_kernel(q_ref, kc_ref, kn_ref, vp_ref, vc_ref, vn_ref,
                cc_ref, sc_ref, cn_ref, sn_ref,
                qw_ref, kw_ref, sink_ref, o_ref, ks_ref):
    i = pl.program_id(0)
    n = pl.num_programs(0)
    batch = q_ref.shape[0]
    lane = lax.broadcasted_iota(jnp.int32, (TILE, LANES), 1)
    lo_half = lane < SWA_HEAD_DIM
    first_quarter = (lane % SWA_HEAD_DIM) < (SWA_HEAD_DIM // 2)

    sub = lax.broadcasted_iota(jnp.int32, (LANES, LANES), 0)
    head_ones = ((sub // SWA_HEAD_DIM) == (lane // SWA_HEAD_DIM)).astype(BF16)

    def prep(x, w, cos_ref, sin_ref):
        return _pair_rope(_pair_rms(x, w, head_ones), cos_ref[...], sin_ref[...], first_quarter)

    kw = kw_ref[...]

    @pl.when(i == 0)
    def _():
        for b in range(batch):
            ks_ref[3 * b] = jnp.zeros((TILE, LANES), BF16)
            ks_ref[3 * b + 1] = prep(kc_ref[b].astype(F32), kw, cc_ref, sc_ref).astype(BF16)

    @pl.when(i > 0)
    def _():
        for b in range(batch):
            ks_ref[3 * b] = ks_ref[3 * b + 1]
            ks_ref[3 * b + 1] = ks_ref[3 * b + 2]

    for b in range(batch):
        ks_ref[3 * b + 2] = prep(kn_ref[b].astype(F32), kw, cn_ref, sn_ref).astype(BF16)
    span = 3 * TILE
    r = lax.broadcasted_iota(jnp.int32, (TILE, span), 0)
    c = lax.broadcasted_iota(jnp.int32, (TILE, span), 1)
    valid = (jnp.abs(c - WINDOW - r) <= WINDOW)
    valid = valid & ((c >= TILE) | (i > 0)) & ((c < 2 * TILE) | (i < n - 1))
    group = SWA_Q_HEADS // SWA_KV_HEADS
    qw = qw_ref[...]
    scale = SWA_HEAD_DIM ** -0.5
    heads = []
    for b in range(batch):
        k_span = ks_ref[3 * b:3 * b + 3].reshape(3 * TILE, LANES)
        v_span = jnp.concatenate([vp_ref[b], vc_ref[b], vn_ref[b]], axis=0)
        for j in range(SWA_Q_HEADS // 2):
            q_pair = prep(q_ref[b, :, j * LANES:(j + 1) * LANES].astype(F32), qw, cc_ref, sc_ref) * scale
            q_swap = pltpu.roll(q_pair, SWA_HEAD_DIM, 1)
            for half in range(2):
                hq = 2 * j + half
                g = hq // group
                q_al = q_pair if half == g else q_swap
                q_m = jnp.where(lo_half if g == 0 else jnp.logical_not(lo_half), q_al, 0.0).astype(BF16)
                lanes = slice(hq * SWA_HEAD_DIM, (hq + 1) * SWA_HEAD_DIM)
                heads.append(_swa_head(q_m, k_span, v_span, valid, sink_ref[hq], o_ref.at[b], lanes,
                                       half != g))
    while heads:
        alive = []
        for head in heads:
            try:
                next(head)
                alive.append(head)
            except StopIteration:
                pass
        heads = alive


def _swa(proj3d, cos_t, sin_t, qw, kw, sinks):
    b, s, _ = proj3d.shape
    n_tiles = s // TILE
    kblk = COL_SWK // LANES
    vblk = COL_SWV // LANES
    prev = lambda i: jnp.maximum(i - 1, 0)
    nxt = lambda i: jnp.minimum(i + 1, n_tiles - 1)

    def kv_spec(blk, f):
        return pl.BlockSpec((b, TILE, LANES), lambda i: (0, f(i), blk))

    def tab_spec(f):
        return pl.BlockSpec((TILE, LANES), lambda i: (f(i), 0))

    ident = lambda i: i
    row_p = pl.BlockSpec((1, LANES), lambda i: (0, 0))
    return pl.pallas_call(
        _swa_kernel,
        grid=(n_tiles,),
        in_specs=[
            pl.BlockSpec((b, TILE, SWA_WIDTH), lambda i: (0, i, COL_SWQ // SWA_WIDTH)),
            kv_spec(kblk, ident), kv_spec(kblk, nxt),
            kv_spec(vblk, prev), kv_spec(vblk, ident), kv_spec(vblk, nxt),
            tab_spec(ident), tab_spec(ident), tab_spec(nxt), tab_spec(nxt),
            row_p, row_p,
            pl.BlockSpec(memory_space=pltpu.SMEM),
        ],
        out_specs=pl.BlockSpec((b, TILE, SWA_WIDTH), lambda i: (0, i, 0)),
        out_shape=jax.ShapeDtypeStruct((b, s, SWA_WIDTH), BF16),
        scratch_shapes=[pltpu.VMEM((3 * b, TILE, LANES), BF16)],
        compiler_params=pltpu.CompilerParams(
            dimension_semantics=("arbitrary",), vmem_limit_bytes=VMEM_LIMIT),
        name="swa",
    )(proj3d, proj3d, proj3d, proj3d, proj3d, proj3d,
      cos_t, sin_t, cos_t, sin_t, qw, kw, sinks)


def _out_kernel(of_ref, ob_ref, z_ref, sw_ref, swz_ref, x_ref, onw_ref, wo_ref, y_ref):
    dn = of_ref[...].astype(F32) + ob_ref[...].astype(F32)
    onw = onw_ref[...]
    parts = []
    for h in range(DN_HEADS):
        blk = dn[:, h * LANES:(h + 1) * LANES]
        ms = jnp.mean(blk * blk, axis=-1, keepdims=True)
        parts.append((blk * lax.rsqrt(ms + EPS)) * onw)
    dn_n = jnp.concatenate(parts, axis=1) * _silu(z_ref[...].astype(F32))
    sw = sw_ref[...].astype(F32) * _silu(swz_ref[...].astype(F32))
    mix = jnp.concatenate([dn_n, sw], axis=1).astype(BF16)
    y_ref[...] = x_ref[...] + jnp.dot(mix, wo_ref[...], preferred_element_type=F32)


def _out_projection(o_f, o_b, proj2d, sw, x2d, out_norm_w, w_out):
    rows = x2d.shape[0]
    half = lambda i: (i, 0)
    return pl.pallas_call(
        _out_kernel,
        grid=(rows // OUT_ROWS,),
        in_specs=[
            pl.BlockSpec((OUT_ROWS, DN_WIDTH), half),
            pl.BlockSpec((OUT_ROWS, DN_WIDTH), half),
            pl.BlockSpec((OUT_ROWS, DN_WIDTH), lambda i: (i, COL_DNZ // DN_WIDTH)),
            pl.BlockSpec((OUT_ROWS, SWA_WIDTH), half),
            pl.BlockSpec((OUT_ROWS, SWA_WIDTH), lambda i: (i, COL_SWZ // SWA_WIDTH)),
            pl.BlockSpec((OUT_ROWS, D_MODEL), half),
            pl.BlockSpec((1, LANES), lambda i: (0, 0)),
            pl.BlockSpec((DN_WIDTH + SWA_WIDTH, D_MODEL), lambda i: (0, 0)),
        ],
        out_specs=pl.BlockSpec((OUT_ROWS, D_MODEL), half),
        out_shape=jax.ShapeDtypeStruct((rows, D_MODEL), F32),
        compiler_params=pltpu.CompilerParams(
            dimension_semantics=("arbitrary",), vmem_limit_bytes=VMEM_LIMIT),
        name="out_proj",
    )(o_f, o_b, proj2d, sw, proj2d, x2d, out_norm_w, w_out)


def _rope_tables(seq):
    inv_freq = ROPE_THETA ** (-jnp.arange(0, SWA_HEAD_DIM, 2, dtype=F32) / SWA_HEAD_DIM)
    ang = jnp.arange(seq, dtype=F32)[:, None] * inv_freq[None, :]
    half = SWA_HEAD_DIM // 2
    lane = jnp.arange(LANES)
    expand = (lane[None, :] % half == jnp.arange(half)[:, None]).astype(F32)
    sign = jnp.where(lane % SWA_HEAD_DIM < half, -1.0, 1.0).astype(F32)
    cos = jnp.dot(jnp.cos(ang), expand, precision=lax.Precision.HIGHEST)
    sin = jnp.dot(jnp.sin(ang), expand, precision=lax.Precision.HIGHEST)
    return cos, sin * sign[None, :]


def _layer(x, norm_w, w_in, conv_w, a_log, dt_bias, out_norm_w, q_norm_w, k_norm_w, sinks, w_out):
    b, s, _ = x.shape
    x2d = x.reshape(b * s, D_MODEL)
    o_q, o_k, o_v, o_z = 0, DN_WIDTH, 2 * DN_WIDTH, 3 * DN_WIDTH
    o_beta = 4 * DN_WIDTH
    o_decay = o_beta + N_GATE
    o_swq = o_decay + N_GATE
    o_swk = o_swq + SWA_WIDTH
    o_swv = o_swk + SWA_KV_WIDTH
    o_swz = o_swv + SWA_KV_WIDTH
    w_gate = w_in[:, o_beta:o_swq]
    w_perm = jnp.concatenate([
        w_in[:, o_q:o_beta],
        w_in[:, o_swq:o_swk], w_in[:, o_swz:o_swz + SWA_WIDTH],
        w_in[:, o_swk:o_swv], w_in[:, o_swv:o_swz]], axis=1).astype(BF16)
    w_gate_pad = jnp.concatenate(
        [w_gate, jnp.zeros((D_MODEL, LANES - 2 * N_GATE), w_in.dtype)], axis=1).astype(BF16)
    conv_pad = jnp.concatenate([conv_w, jnp.zeros((SUBLANES - DN_CONV, 3 * DN_WIDTH), conv_w.dtype)], axis=0)
    proj2d, gate_c, gate_t, q, k, v = _in_projection_conv(
        x2d, s, norm_w.reshape(1, D_MODEL), w_perm, w_gate_pad, w_gate.T.astype(BF16), conv_pad)
    proj3d = proj2d.reshape(b, s, PROJ_WIDTH - 3 * DN_WIDTH)
    q, k, v = (a.reshape(b, s, DN_WIDTH) for a in (q, k, v))

    a_flat = a_log.reshape(N_GATE).astype(F32)
    d_flat = dt_bias.reshape(N_GATE).astype(F32)
    zeros8 = jnp.zeros((N_GATE,), F32)
    pad_row = jnp.zeros((LANES - 2 * N_GATE,), F32)
    al_row = jnp.concatenate([zeros8, a_flat, pad_row]).reshape(1, LANES)
    dt_row = jnp.concatenate([zeros8, d_flat, pad_row]).reshape(1, LANES)
    al_col = jnp.concatenate([zeros8, a_flat]).reshape(2 * N_GATE, 1)
    dt_col = jnp.concatenate([zeros8, d_flat]).reshape(2 * N_GATE, 1)
    o_f, o_b = _dn_scan(q, k, v, gate_c.reshape(b, s, LANES), gate_t, al_row, dt_row, al_col, dt_col)

    cos_t, sin_t = _rope_tables(s)
    reps = LANES // SWA_HEAD_DIM
    sw = _swa(proj3d, cos_t, sin_t,
              jnp.tile(q_norm_w.astype(F32), reps).reshape(1, LANES),
              jnp.tile(k_norm_w.astype(F32), reps).reshape(1, LANES),
              sinks.astype(F32))

    y = _out_projection(o_f.reshape(b * s, DN_WIDTH), o_b.reshape(b * s, DN_WIDTH), proj2d,
                        sw.reshape(b * s, SWA_WIDTH), x2d,
                        out_norm_w.reshape(1, LANES).astype(F32), w_out.astype(BF16))
    return y.reshape(b, s, D_MODEL)


def kernel(x, norm_w, w_in, dn_conv_w, dn_a_log, dn_dt_bias, dn_out_norm_w,
           swa_q_norm_w, swa_k_norm_w, swa_sinks, w_out):
    for l in range(norm_w.shape[0]):
        x = _layer(x, norm_w[l], w_in[l], dn_conv_w[l], dn_a_log[l], dn_dt_bias[l], dn_out_norm_w[l],
                   swa_q_norm_w[l], swa_k_norm_w[l], swa_sinks[l], w_out[l])
    return x
```
